```python
import jax
import jax.numpy as jnp
from jax import lax
import numpy as np

D_MODEL = 1024
BATCH = 8
SEQ = 8192
DEPTH = 2

CTX_LEN = 256
GRID_W = 64
N_EVEN = (DEPTH + 1) // 2
N_ODD = DEPTH // 2
CONV_W = D_MODEL
CONV_K = 31
POOL_WINDOWS = (2, 4, 8, 16)
POOL_GROUPS = 4
POOL_W = D_MODEL
POOL_GC = POOL_W // POOL_GROUPS
AB_IN = 2 * CONV_W + POOL_W
AB_OUT = CONV_W + POOL_W
MLSTM_E = 2 * D_MODEL
MLSTM_H = 4
MLSTM_DH = MLSTM_E // MLSTM_H
QKV_BLOCK = 4
SHORT_K = 4
CHUNK = 64
N_EXPERTS = 16
N_GROUPS = 4
EXPERTS_PER_GROUP = N_EXPERTS // N_GROUPS
TOP_K = 2
D_EXPERT = D_MODEL
MOE_BLOCK = 128
EPS = 1e-6

kernel_name = 'hybrid_conv_pool_mlstm_moe_prefix_block'


def rmsnorm(x, g):
    x32 = x.astype(jnp.float32)
    y = x32 * lax.rsqrt(jnp.mean(x32 * x32, -1, keepdims=True) + EPS) * g.astype(jnp.float32)
    return y.astype(x.dtype)


def layernorm(x, g, b):
    x32 = x.astype(jnp.float32)
    mu = jnp.mean(x32, -1, keepdims=True)
    var = jnp.mean(jnp.square(x32 - mu), -1, keepdims=True)
    y = (x32 - mu) * lax.rsqrt(var + EPS) * g.astype(jnp.float32) + b.astype(jnp.float32)
    return y.astype(x.dtype)


def modulate(x, shift, scale):
    return x * (1 + scale) + shift


def dwconv(x, w, b, pad):
    y = lax.conv_general_dilated(x, w[:, None, :].astype(x.dtype), window_strides=(1,), padding=[pad],
                                 dimension_numbers=('NWC', 'WIO', 'NWC'), feature_group_count=x.shape[-1])
    return y + b


def centred_mean(x, window):
    L = x.shape[1]
    left = window // 2
    right = window - 1 - left
    cs = jnp.cumsum(x.astype(jnp.float32), axis=1)
    cs = jnp.concatenate([jnp.zeros_like(cs[:, :1]), cs], axis=1)
    pos = jnp.arange(L)
    lo = jnp.clip(pos - left, 0, L)
    hi = jnp.clip(pos + right + 1, 0, L)
    total = jnp.take(cs, hi, axis=1) - jnp.take(cs, lo, axis=1)
    cnt = (hi - lo).astype(jnp.float32).reshape((1, L) + (1,) * (x.ndim - 2))
    return (total / cnt).astype(x.dtype)


def mixer_ab(h, grid, w_in, w_dw, b_dw, ln_g, ln_b, w_pool, b_pool, pool_scale, w_out):
    B, L, _ = h.shape
    u = h @ w_in
    a = u[..., :CONV_W] * jax.nn.sigmoid(u[..., CONV_W:2 * CONV_W])
    bp = u[..., 2 * CONV_W:]
    pad = (CONV_K // 2, CONV_K // 2)
    if grid:
        rows = L // GRID_W
        a = dwconv(a.reshape(B * rows, GRID_W, CONV_W), w_dw, b_dw, pad).reshape(B, L, CONV_W)
        bp = bp.reshape(B, rows, GRID_W, POOL_GROUPS, POOL_GC)
    else:
        a = dwconv(a, w_dw, b_dw, pad)
        bp = bp.reshape(B, L, POOL_GROUPS, POOL_GC)
    a = jax.nn.silu(layernorm(a, ln_g, ln_b))
    pooled = jnp.stack([centred_mean(bp[..., g, :], w) - bp[..., g, :] for g, w in enumerate(POOL_WINDOWS)], axis=-2)
    p = jnp.einsum('...gc,gcd->...gd', pooled, w_pool) + b_pool
    p = p.reshape(B, L, POOL_W) * pool_scale
    return jnp.concatenate([a, p], axis=-1) @ w_out


def blockdiag(x, w):
    B, L, E = x.shape
    return jnp.einsum('blni,nio->blno', x.reshape(B, L, E // QKV_BLOCK, QKV_BLOCK), w).reshape(B, L, E)


def mlstm_inputs(h, w_in, w_conv, b_conv, w_q, w_k, w_v, w_gates, b_gates):
    B, L, _ = h.shape
    u = h @ w_in
    xm, z = u[..., :MLSTM_E], u[..., MLSTM_E:]
    xc = jax.nn.silu(dwconv(xm, w_conv, b_conv, (SHORT_K // 2, SHORT_K - 1 - SHORT_K // 2)))
    q = blockdiag(xc, w_q)
    k = blockdiag(xc, w_k)
    v = blockdiag(xm, w_v)
    g = (jnp.concatenate([q, k, v], axis=-1) @ w_gates + b_gates).astype(jnp.float32).reshape(B, L, 4, MLSTM_H)
    heads = lambda t: t.reshape(B, L, MLSTM_H, MLSTM_DH)
    return heads(q), heads(k) * (MLSTM_DH ** -0.5), heads(v), g, jax.nn.sigmoid(z)


def mlstm_scan(q, k, v, i_pre, f_pre, state):
    B, L, H, DH = q.shape
    nc = L // CHUNK
    chunks = lambda t: t.astype(jnp.float32).reshape(B, nc, CHUNK, H, -1).transpose(1, 0, 3, 2, 4)
    gchunks = lambda t: t.astype(jnp.float32).reshape(B, nc, CHUNK, H).transpose(1, 0, 3, 2)
    tri = jnp.tril(jnp.ones((CHUNK, CHUNK), dtype=bool))

    def step(carry, inp):
        C, n, m = carry
        qc, kc, vc, ic, lf = inp
        b = jnp.cumsum(lf, axis=-1)
        dmat = jnp.where(tri, b[..., :, None] - b[..., None, :] + ic[..., None, :], -jnp.inf)
        inter = b + m[..., None]
        m_t = jnp.maximum(inter, dmat.max(-1))
        wmat = jnp.exp(dmat - m_t[..., None])
        w_inter = jnp.exp(inter - m_t)
        s = jnp.einsum('bhtd,bhsd->bhts', qc, kc) * wmat
        num = jnp.einsum('bhts,bhse->bhte', s, vc) + w_inter[..., None] * jnp.einsum('bhtd,bhde->bhte', qc, C)
        den = s.sum(-1) + w_inter * jnp.einsum('bhtd,bhd->bht', qc, n)
        h = num / jnp.maximum(jnp.abs(den), jnp.exp(-m_t))[..., None]
        b_end = b[..., -1]
        dlog = b_end[..., None] - b + ic
        m_new = jnp.maximum(b_end + m, dlog.max(-1))
        wk = jnp.exp(dlog - m_new[..., None])
        decay = jnp.exp(b_end + m - m_new)
        C = decay[..., None, None] * C + jnp.einsum('bhsd,bhse->bhde', kc * wk[..., None], vc)
        n = decay[..., None] * n + jnp.einsum('bhs,bhsd->bhd', wk, kc)
        return (C, n, m_new), h

    state, hs = lax.scan(step, state, (chunks(q), chunks(k), chunks(v), gchunks(i_pre),
                                       gchunks(jax.nn.log_sigmoid(f_pre))))
    return hs.transpose(1, 0, 3, 2, 4).reshape(B, L, H, DH), state


def mlstm_out(hsum, o, hn_g, w_out):
    B, L, H, DH = hsum.shape
    mu = jnp.mean(hsum, -1, keepdims=True)
    var = jnp.mean(jnp.square(hsum - mu), -1, keepdims=True)
    hn = ((hsum - mu) * lax.rsqrt(var + EPS)).reshape(B, L, H * DH) * hn_g.astype(jnp.float32)
    return (o * hn.astype(o.dtype)) @ w_out


def mixer_c(h_lat, h_ctx, need_ctx, w_in, w_conv, b_conv, w_q, w_k, w_v, w_gates, b_gates, hn_g, w_out):
    prm = (w_in, w_conv, b_conv, w_q, w_k, w_v, w_gates, b_gates)
    ql, kl, vl, gl, ol = mlstm_inputs(h_lat, *prm)
    qc, kc, vc, gc, oc = mlstm_inputs(h_ctx, *prm)
    B = h_lat.shape[0]
    s0 = (jnp.zeros((B, MLSTM_H, MLSTM_DH, MLSTM_DH), jnp.float32),
          jnp.zeros((B, MLSTM_H, MLSTM_DH), jnp.float32),
          jnp.zeros((B, MLSTM_H), jnp.float32))
    flip = lambda t: jnp.flip(t, axis=1)
    hcf, sf = mlstm_scan(qc, kc, vc, gc[:, :, 0], gc[:, :, 1], s0)
    hcb, sb = mlstm_scan(flip(qc), flip(kc), flip(vc), flip(gc[:, :, 2]), flip(gc[:, :, 3]), s0)
    hlf, _ = mlstm_scan(ql, kl, vl, gl[:, :, 0], gl[:, :, 1], sf)
    hlb, _ = mlstm_scan(flip(ql), flip(kl), flip(vl), flip(gl[:, :, 2]), flip(gl[:, :, 3]), sb)
    y_lat = mlstm_out(hlf + flip(hlb), ol, hn_g, w_out)
    y_ctx = mlstm_out(hcf + flip(hcb), oc, hn_g, w_out) if need_ctx else None
    return y_lat, y_ctx


def moe(h, w_router, b_router, w1, w3, w2):
    N, D = h.shape
    scores = jax.nn.sigmoid(h.astype(jnp.float32) @ w_router.astype(jnp.float32))
    biased = (scores + b_router.astype(jnp.float32)).reshape(N, N_GROUPS, EXPERTS_PER_GROUP)
    gscore = lax.top_k(biased, 2)[0].sum(-1)
    gsel = lax.top_k(gscore, 1)[1]
    cand = jnp.take_along_axis(biased, gsel[:, :, None], axis=1)[:, 0]
    loc = lax.top_k(cand, TOP_K)[1]
    eidx = gsel * EXPERTS_PER_GROUP + loc
    wts = jnp.take_along_axis(scores, eidx, axis=1)
    wts = wts / wts.sum(-1, keepdims=True)
    A = N * TOP_K
    flat_e = eidx.reshape(A)
    order = jnp.argsort(flat_e)
    e_s = flat_e[order]
    tok_s = jnp.repeat(jnp.arange(N), TOP_K)[order]
    w_s = wts.reshape(A)[order]
    counts = jnp.bincount(flat_e, length=N_EXPERTS)
    padded = (counts + MOE_BLOCK - 1) // MOE_BLOCK * MOE_BLOCK
    pad_end = jnp.cumsum(padded)
    pad_start = pad_end - padded
    start = jnp.cumsum(counts) - counts
    dest = pad_start[e_s] + jnp.arange(A) - start[e_s]
    P = ((A + MOE_BLOCK - 1) // MOE_BLOCK + N_EXPERTS) * MOE_BLOCK
    nb = P // MOE_BLOCK
    buf = jnp.zeros((P, D), h.dtype).at[dest].set(h[tok_s])
    block_e = jnp.minimum(jnp.searchsorted(pad_end, jnp.arange(nb) * MOE_BLOCK, side='right'), N_EXPERTS - 1)

    def expert_block(args):
        xb, e = args
        return (jax.nn.silu(xb @ w1[e]) * (xb @ w3[e])) @ w2[e]

    out = lax.map(expert_block, (buf.reshape(nb, MOE_BLOCK, D), block_e)).reshape(P, D)
    return jnp.zeros((N, D), h.dtype).at[tok_s].add(out[dest] * w_s[:, None].astype(h.dtype))


def setup_inputs(seed: int = 0) -> dict:
    key = jax.random.key(seed)
    ks = list(jax.random.split(key, 40))

    def nrm(i, shape, scale):
        return jax.random.normal(ks[i], shape, jnp.float32) * scale

    D = D_MODEL
    NE, NO, E, H = N_EVEN, N_ODD, MLSTM_E, MLSTM_H
    i_bias = nrm(23, (NO, 2, 1, H), 0.1)
    f_bias = jnp.linspace(3.0, 6.0, H, dtype=jnp.float32) + nrm(24, (NO, 2, 1, H), 0.1)
    b_gates = jnp.concatenate([i_bias, f_bias], axis=2).reshape(NO, 4 * H)
    qkv_shape = (NO, E // QKV_BLOCK, QKV_BLOCK, QKV_BLOCK)
    return {
        'x': nrm(0, (BATCH, SEQ, D), 1.0),
        'c': nrm(1, (BATCH, D), 1.0),
        'ctx': nrm(2, (BATCH, CTX_LEN, D), 1.0),
        'c_ctx': nrm(3, (D,), 1.0),
        'w_ada': nrm(4, (DEPTH, D, 6 * D), 0.5 * D ** -0.5),
        'b_ada': nrm(5, (DEPTH, 6 * D), 0.02),
        'g_mix': 1.0 + nrm(6, (DEPTH, D), 0.05),
        'g_ffn': 1.0 + nrm(7, (DEPTH, D), 0.05),
        'g_final': 1.0 + nrm(8, (D,), 0.05),
        'w_in_ab': nrm(9, (NE, D, AB_IN), D ** -0.5),
        'w_dw_a': nrm(10, (NE, CONV_K, CONV_W), CONV_K ** -0.5),
        'b_dw_a': nrm(11, (NE, CONV_W), 0.02),
        'ln_g_a': 1.0 + nrm(12, (NE, CONV_W), 0.05),
        'ln_b_a': nrm(13, (NE, CONV_W), 0.02),
        'w_pool': nrm(14, (NE, POOL_GROUPS, POOL_GC, POOL_GC), POOL_GC ** -0.5),
        'b_pool': nrm(15, (NE, POOL_GROUPS, POOL_GC), 0.02),
        'pool_scale': 1.0 + nrm(16, (NE, POOL_W), 0.1),
        'w_out_ab': nrm(17, (NE, AB_OUT, D), AB_OUT ** -0.5),
        'w_in_c': nrm(18, (NO, D, 2 * E), D ** -0.5),
        'w_conv_c': nrm(19, (NO, SHORT_K, E), SHORT_K ** -0.5),
        'b_conv_c': nrm(20, (NO, E), 0.02),
        'w_q': nrm(21, qkv_shape, QKV_BLOCK ** -0.5),
        'w_k': nrm(22, qkv_shape, QKV_BLOCK ** -0.5),
        'w_v': nrm(25, qkv_shape, QKV_BLOCK ** -0.5),
        'w_gates': nrm(26, (NO, 3 * E, 4 * H), (3 * E) ** -0.5),
        'b_gates': b_gates,
        'hn_g': 1.0 + nrm(27, (NO, E), 0.05),
        'w_out_c': nrm(28, (NO, E, D), E ** -0.5),
        'w_router': nrm(29, (D, N_EXPERTS), D ** -0.5),
        'b_router': nrm(30, (N_EXPERTS,), 0.01),
        'w1': nrm(31, (DEPTH, N_EXPERTS, D, D_EXPERT), D ** -0.5),
        'w3': nrm(32, (DEPTH, N_EXPERTS, D, D_EXPERT), D ** -0.5),
        'w2': nrm(33, (DEPTH, N_EXPERTS, D_EXPERT, D), D_EXPERT ** -0.5),
    }


def reference(x, c, ctx, c_ctx, w_ada, b_ada, g_mix, g_ffn, g_final, w_in_ab, w_dw_a, b_dw_a, ln_g_a, ln_b_a,
              w_pool, b_pool, pool_scale, w_out_ab, w_in_c, w_conv_c, b_conv_c, w_q, w_k, w_v, w_gates, b_gates,
              hn_g, w_out_c, w_router, b_router, w1, w3, w2):
    B, S, D = x.shape
    n_ctx = B * ctx.shape[1]
    x_lat, x_ctx = x, ctx
    sc = jax.nn.silu(c)
    scc = jax.nn.silu(c_ctx)
    for l in range(DEPTH):
        last = l == DEPTH - 1
        even = l % 2 == 0
        j = l // 2
        mod_lat = jnp.split((sc @ w_ada[l] + b_ada[l])[:, None, :], 6, axis=-1)
        mod_ctx = jnp.split((scc @ w_ada[l] + b_ada[l])[None, None, :], 6, axis=-1)
        h_lat = modulate(rmsnorm(x_lat, g_mix[l]), mod_lat[0], mod_lat[1])
        if even:
            prm = (w_in_ab[j], w_dw_a[j], b_dw_a[j], ln_g_a[j], ln_b_a[j], w_pool[j], b_pool[j], pool_scale[j], w_out_ab[j])
            y_lat = mixer_ab(h_lat, True, *prm)
            if not last:
                h_ctx = modulate(rmsnorm(x_ctx, g_mix[l]), mod_ctx[0], mod_ctx[1])
                y_ctx = mixer_ab(h_ctx, False, *prm)
        else:
            h_ctx = modulate(rmsnorm(x_ctx, g_mix[l]), mod_ctx[0], mod_ctx[1])
            y_lat, y_ctx = mixer_c(h_lat, h_ctx, not last, w_in_c[j], w_conv_c[j], b_conv_c[j], w_q[j], w_k[j],
                                   w_v[j], w_gates[j], b_gates[j], hn_g[j], w_out_c[j])
        x_lat = x_lat + mod_lat[2] * y_lat
        h_lat = modulate(rmsnorm(x_lat, g_ffn[l]), mod_lat[3], mod_lat[4])
        if last:
            y = moe(h_lat.reshape(B * S, D), w_router, b_router, w1[l], w3[l], w2[l])
            x_lat = x_lat + mod_lat[5] * y.reshape(B, S, D)
        else:
            x_ctx = x_ctx + mod_ctx[2] * y_ctx
            h_ctx = modulate(rmsnorm(x_ctx, g_ffn[l]), mod_ctx[3], mod_ctx[4])
            y = moe(jnp.concatenate([h_ctx.reshape(n_ctx, D), h_lat.reshape(B * S, D)], axis=0),
                    w_router, b_router, w1[l], w3[l], w2[l])
            x_ctx = x_ctx + mod_ctx[5] * y[:n_ctx].reshape(x_ctx.shape)
            x_lat = x_lat + mod_lat[5] * y[n_ctx:].reshape(B, S, D)
    return rmsnorm(x_lat, g_final)
```

```python
import functools

import jax
import jax.numpy as jnp
from jax import lax
from jax.experimental import pallas as pl
from jax.experimental.pallas import tpu as pltpu

F32 = jnp.float32
BF16 = jnp.bfloat16
I32 = jnp.int32
HIGHEST = lax.Precision.HIGHEST

D = 1024
EPS = 1e-6
GRID_W = 64
CONV_K = 31
POOL_WINDOWS = (2, 4, 8, 16)
POOL_GC = 256
E = 2 * D
NH = 4
DH = E // NH
SHORT_K = 4
N_EXPERTS = 16
EPG = 4
N_CLASSES = 24
PAIR_LO = (0, 0, 0, 1, 1, 2)
PAIR_HI = (1, 2, 3, 2, 3, 3)

LANES = 128
MXU_DIM = 256
TM = 512
TM1 = 256
CHUNK_T = 256
MOE_TB = 256
META_W = LANES
VMEM_LIMIT = 56 * 1024 * 1024


def _cparams(sem):
    return pltpu.CompilerParams(dimension_semantics=sem, vmem_limit_bytes=VMEM_LIMIT)


def _sigmoid(x):
    return jax.nn.sigmoid(x)


def _rms_mod(x, g, shift, scale):
    ms = jnp.mean(x * x, axis=-1, keepdims=True)
    return (x * lax.rsqrt(ms + EPS) * g) * (1.0 + scale) + shift


def _mod_kernel(s_ref, w_ref, b_ref, o_ref):
    s = s_ref[...]
    s = s * _sigmoid(s)
    o_ref[0] = jnp.dot(s, w_ref[0], preferred_element_type=F32, precision=HIGHEST) + b_ref[0]


def _mod_call(s16, w_ada, b_ada):
    depth = w_ada.shape[0]
    tn = 1536
    return pl.pallas_call(
        _mod_kernel,
        grid=(depth, 6 * D // tn),
        in_specs=[
            pl.BlockSpec((16, D), lambda l, n: (0, 0)),
            pl.BlockSpec((1, D, tn), lambda l, n: (l, 0, n)),
            pl.BlockSpec((1, 1, tn), lambda l, n: (l, 0, n)),
        ],
        out_specs=pl.BlockSpec((1, 16, tn), lambda l, n: (l, 0, n)),
        out_shape=jax.ShapeDtypeStruct((depth, 16, 6 * D), F32),
        compiler_params=_cparams(("parallel", "parallel")),
        name="mod",
    )(s16, w_ada, b_ada.reshape(depth, 1, 6 * D))


def _a1_kernel(x_ref, mod_ref, g_ref, win_ref, wdw_ref, bdw_ref, lng_ref, lnb_ref,
               a_ref, bp_ref, pad_ref, conv_ref, *, seg):
    tm = x_ref.shape[0]
    nseg = tm // seg
    h = _rms_mod(x_ref[...], g_ref[...], mod_ref[0, 0:1, :], mod_ref[0, 1:2, :])
    u = jnp.dot(h.astype(BF16), win_ref[...], preferred_element_type=F32)
    a = u[:, :D] * _sigmoid(u[:, D:2 * D])
    bp_ref[...] = u[:, 2 * D:]
    zeros16 = jnp.zeros((nseg, 16, D), F32)
    pad_ref[:, 0:16, :] = zeros16
    pad_ref[:, seg + 16:seg + 32, :] = zeros16
    pad_ref[:, 16:16 + seg, :] = a.reshape(nseg, seg, D)

    def seg_body(s, carry):
        for cs in range(D // LANES):
            sl = slice(cs * LANES, (cs + 1) * LANES)
            acc = jnp.zeros((seg, LANES), F32)
            for k in range(CONV_K):
                acc = acc + wdw_ref[k:k + 1, sl] * pad_ref[s, pl.ds(1 + k, seg), sl]
            conv_ref[pl.ds(pl.multiple_of(s * seg, seg), seg), sl] = acc + bdw_ref[:, sl]
        return carry

    lax.fori_loop(0, nseg, seg_body, 0)
    cv = conv_ref[...]
    mu = jnp.mean(cv, axis=-1, keepdims=True)
    cc = cv - mu
    var = jnp.mean(cc * cc, axis=-1, keepdims=True)
    y = cc * lax.rsqrt(var + EPS) * lng_ref[...] + lnb_ref[...]
    a_ref[...] = (y * _sigmoid(y)).astype(BF16)


def _a1_call(x2d, mod, g, win, wdw, bdw, lng, lnb, *, seq, tm, seg):
    n = x2d.shape[0]
    tps = seq // tm
    row = lambda v: v.reshape(1, -1)
    const = lambda shape: pl.BlockSpec(shape, lambda i: (0,) * len(shape))
    return pl.pallas_call(
        functools.partial(_a1_kernel, seg=seg),
        grid=(n // tm,),
        in_specs=[
            pl.BlockSpec((tm, D), lambda i: (i, 0)),
            pl.BlockSpec((1, 8, D), lambda i: ((i // tps) % mod.shape[0], 0, 0)),
            const((1, D)), const((D, 3 * D)), const((CONV_K, D)), const((1, D)), const((1, D)), const((1, D)),
        ],
        out_specs=[pl.BlockSpec((tm, D), lambda i: (i, 0)), pl.BlockSpec((tm, D), lambda i: (i, 0))],
        out_shape=[jax.ShapeDtypeStruct((n, D), BF16), jax.ShapeDtypeStruct((n, D), F32)],
        scratch_shapes=[pltpu.VMEM((tm // seg, seg + 32, D), F32), pltpu.VMEM((tm, D), F32)],
        compiler_params=_cparams(("parallel",)),
        name="a1",
    )(x2d, mod, row(g), win, wdw, row(bdw), row(lng), row(lnb))


def _tail(x, y, mod_ref, gffn_ref, wrT_ref, br_ref, x1_ref, h2_ref, cls_ref, meta_ref):
    tm = x.shape[0]
    x1 = x + mod_ref[0, 2:3, :] * y
    x1_ref[...] = x1
    h2 = _rms_mod(x1, gffn_ref[...], mod_ref[0, 3:4, :], mod_ref[0, 4:5, :])
    logits = lax.dot_general(wrT_ref[...], h2, (((1,), (1,)), ((), ())),
                             precision=HIGHEST, preferred_element_type=F32)
    sc = _sigmoid(logits)
    bi = sc + br_ref[...]
    gs = []
    for g in range(N_EXPERTS // EPG):
        v = [bi[EPG * g + j:EPG * g + j + 1, :] for j in range(EPG)]
        best = v[0] + v[1]
        for (p, q) in ((0, 2), (0, 3), (1, 2), (1, 3), (2, 3)):
            best = jnp.maximum(best, v[p] + v[q])
        gs.append(best)
    gsel = jnp.zeros((1, tm), I32)
    gbest = gs[0]
    for g in range(1, N_EXPERTS // EPG):
        better = gs[g] > gbest
        gsel = jnp.where(better, g, gsel)
        gbest = jnp.where(better, gs[g], gbest)

    def pick(arr, j):
        out = arr[j:j + 1, :]
        for g in range(1, N_EXPERTS // EPG):
            out = jnp.where(gsel == g, arr[EPG * g + j:EPG * g + j + 1, :], out)
        return out

    cand = [pick(bi, j) for j in range(EPG)]
    scs = [pick(sc, j) for j in range(EPG)]
    i1 = jnp.zeros((1, tm), I32)
    b1 = cand[0]
    for j in range(1, EPG):
        better = cand[j] > b1
        i1 = jnp.where(better, j, i1)
        b1 = jnp.where(better, cand[j], b1)
    neg = jnp.full((1, tm), -jnp.inf, F32)
    i2 = jnp.zeros((1, tm), I32)
    b2 = neg
    for j in range(EPG):
        cj = jnp.where(i1 == j, neg, cand[j])
        better = cj > b2
        i2 = jnp.where(better, j, i2)
        b2 = jnp.where(better, cj, b2)
    w1 = jnp.zeros((1, tm), F32)
    w2 = jnp.zeros((1, tm), F32)
    for j in range(EPG):
        w1 = jnp.where(i1 == j, scs[j], w1)
        w2 = jnp.where(i2 == j, scs[j], w2)
    den = w1 + w2
    w1 = w1 / den
    w2 = w2 / den
    first_lo = i1 < i2
    lo = jnp.minimum(i1, i2)
    hi = jnp.maximum(i1, i2)
    w_lo = jnp.where(first_lo, w1, w2)
    w_hi = jnp.where(first_lo, w2, w1)
    pidx = jnp.where(lo == 0, hi - 1, jnp.where(lo == 1, hi + 1, 5))
    cls = gsel * len(PAIR_LO) + pidx
    cls_ref[...] = jnp.broadcast_to(cls, (8, tm))
    meta_ref[...] = jnp.zeros((META_W, tm), F32)
    meta_ref[0:1, :] = w_lo
    meta_ref[1:2, :] = w_hi
    h2_ref[:, 0:D] = h2
    h2_ref[:, D:D + META_W] = meta_ref[...].T


def _a2_kernel(bpp_ref, bpc_ref, bpn_ref, a_ref, x_ref, mod_ref, wpool_ref, bpool_ref, pscale_ref, wo_ref,
               gffn_ref, wrT_ref, br_ref, x1_ref, h2_ref, cls_ref, pb_ref, meta_ref, *, stride, npos, use_halo):
    tm = x_ref.shape[0]
    halo = 8 * stride
    i = pl.program_id(1)
    nt = pl.num_programs(1)
    cur = bpc_ref[...]
    if use_halo:
        pb_ref[0:halo, :] = jnp.where(i > 0, bpp_ref[...], 0.0)
        pb_ref[halo + tm:halo + tm + halo, :] = jnp.where(i < nt - 1, bpn_ref[...], 0.0)
    else:
        pb_ref[0:halo, :] = jnp.zeros((halo, D), F32)
        pb_ref[halo + tm:halo + tm + halo, :] = jnp.zeros((halo, D), F32)
    pb_ref[halo:halo + tm, :] = cur
    pos = (i * tm + lax.broadcasted_iota(I32, (tm, 1), 0)) // stride
    y = jnp.dot(a_ref[...], wo_ref[0:D, :], preferred_element_type=F32)
    for g, w in enumerate(POOL_WINDOWS):
        gsl = slice(g * POOL_GC, (g + 1) * POOL_GC)
        left = w // 2
        right = w - 1 - left
        tot = jnp.zeros((tm, POOL_GC), F32)
        for j in range(-left, right + 1):
            tot = tot + pb_ref[halo + j * stride:halo + j * stride + tm, gsl]
        cnt = (jnp.minimum(pos + right + 1, npos) - jnp.maximum(pos - left, 0)).astype(F32)
        pooled = tot / cnt - cur[:, gsl]
        pg = jnp.dot(pooled.astype(BF16), wpool_ref[g], preferred_element_type=F32) + bpool_ref[g:g + 1, :]
        pg = pg * pscale_ref[:, gsl]
        y = y + jnp.dot(pg.astype(BF16), wo_ref[D + g * POOL_GC:D + (g + 1) * POOL_GC, :],
                        preferred_element_type=F32)
    _tail(x_ref[...], y, mod_ref, gffn_ref, wrT_ref, br_ref, x1_ref, h2_ref, cls_ref, meta_ref)


def _a2_call(bp, a, x2d, mod, wpool, bpool, pscale, wo, gffn, wrT, br, *, seq, tm, stride, npos, use_halo):
    n = x2d.shape[0]
    nb = n // seq
    tps = seq // tm
    ntile = n // tm
    halo = 8 * stride
    blk = lambda b, i: b * tps + i
    row = lambda v: v.reshape(1, -1)
    const = lambda shape: pl.BlockSpec(shape, lambda b, i: (0,) * len(shape))
    in_specs = [
        pl.BlockSpec((tm, D), lambda b, i: (jnp.maximum(blk(b, i) - 1, 0), 0)),
        pl.BlockSpec((tm, D), lambda b, i: (blk(b, i), 0)),
        pl.BlockSpec((tm, D), lambda b, i: (jnp.minimum(blk(b, i) + 1, ntile - 1), 0)),
        pl.BlockSpec((tm, D), lambda b, i: (blk(b, i), 0)),
        pl.BlockSpec((tm, D), lambda b, i: (blk(b, i), 0)),
        pl.BlockSpec((1, 8, D), lambda b, i: (b % mod.shape[0], 0, 0)),
        const((4, POOL_GC, POOL_GC)), const((4, POOL_GC)), const((1, D)), const((2 * D, D)),
        const((1, D)), const((N_EXPERTS, D)), const((N_EXPERTS, 1)),
    ]
    args = [bp, bp, bp, a, x2d, mod, wpool, bpool, row(pscale), wo, row(gffn), wrT, br.reshape(N_EXPERTS, 1)]
    return pl.pallas_call(
        functools.partial(_a2_kernel, stride=stride, npos=npos, use_halo=use_halo),
        grid=(nb, tps),
        in_specs=in_specs,
        out_specs=[
            pl.BlockSpec((tm, D), lambda b, i: (blk(b, i), 0)),
            pl.BlockSpec((tm, D + META_W), lambda b, i: (blk(b, i), 0)),
            pl.BlockSpec((8, tm), lambda b, i: (0, blk(b, i))),
        ],
        out_shape=[
            jax.ShapeDtypeStruct((n, D), F32),
            jax.ShapeDtypeStruct((n, D + META_W), F32),
            jax.ShapeDtypeStruct((8, n), I32),
        ],
        scratch_shapes=[pltpu.VMEM((tm + 2 * halo, D), F32), pltpu.VMEM((META_W, tm), F32)],
        compiler_params=_cparams(("parallel", "arbitrary")),
        name="a2",
    )(*args)


def _moe_kernel(nused_ref, tlo_ref, thi_ref, tcnt_ref, idx_hbm, h2_hbm, w1lo, w3lo, w2lo, w1hi, w3hi, w2hi, y_hbm,
                idx_smem, xbuf, ybuf, sem_idx, sem_g, sem_s, *, tb):
    i = pl.program_id(0)
    nused = nused_ref[0]
    unroll = 8

    def idx_copy(t, slot):
        return pltpu.make_async_copy(idx_hbm.at[t], idx_smem.at[slot], sem_idx.at[slot])

    def gather_copy(tok, slot2, j):
        return pltpu.make_async_copy(h2_hbm.at[pl.ds(tok, 1), :], xbuf.at[slot2, pl.ds(j, 1), :], sem_g.at[slot2])

    def scatter_copy(dst, slot2, j):
        return pltpu.make_async_copy(ybuf.at[slot2, pl.ds(j, 1), :], y_hbm.at[pl.ds(dst, 1), :], sem_s.at[slot2])

    def rows(fn, count=tb):
        def body(jj, c):
            for u in range(unroll):
                fn(jj * unroll + u)
            return c
        full = count // unroll
        lax.fori_loop(0, full, body, 0)
        if not isinstance(count, int):
            def tail(j, c):
                fn(j)
                return c
            lax.fori_loop(full * unroll, count, tail, 0)

    def start_gather(t):
        s3 = lax.rem(t, 3)
        s2 = lax.rem(t, 2)
        rows(lambda j: gather_copy(idx_smem[s3, 0, j], s2, j).start())

    def wait_gather(t):
        s2 = lax.rem(t, 2)
        rows(lambda j: gather_copy(0, s2, j).wait())

    def start_scatter(t):
        s3 = lax.rem(t, 3)
        s2 = lax.rem(t, 2)
        rows(lambda j: scatter_copy(idx_smem[s3, 0, j], s2, j).start(), tcnt_ref[t])

    def wait_scatter(t):
        s2 = lax.rem(t, 2)
        rows(lambda j: scatter_copy(0, s2, j).wait(), tcnt_ref[t])

    @pl.when(i == 0)
    def _():
        idx_copy(0, 0).start()
        idx_copy(0, 0).wait()
        start_gather(0)

        @pl.when(nused > 1)
        def _():
            idx_copy(1, 1).start()

    @pl.when(i < nused)
    def _():
        @pl.when(i + 1 < nused)
        def _():
            idx_copy(i + 1, lax.rem(i + 1, 3)).wait()
            start_gather(i + 1)

        @pl.when(i + 2 < nused)
        def _():
            idx_copy(i + 2, lax.rem(i + 2, 3)).start()

        wait_gather(i)
        s2 = lax.rem(i, 2)
        x = xbuf[s2]
        xb = x[:, 0:D].astype(BF16)

        def ffn(w1, w3, w2):
            h1 = jnp.dot(xb, w1[0], preferred_element_type=F32)
            h3 = jnp.dot(xb, w3[0], preferred_element_type=F32)
            act = (h1 * _sigmoid(h1)) * h3
            return jnp.dot(act.astype(BF16), w2[0], preferred_element_type=F32)

        y = x[:, D:D + 1] * ffn(w1lo, w3lo, w2lo) + x[:, D + 1:D + 2] * ffn(w1hi, w3hi, w2hi)

        @pl.when(i >= 2)
        def _():
            wait_scatter(i - 2)

        ybuf[s2] = y
        start_scatter(i)

        @pl.when(i == nused - 1)
        def _():
            @pl.when(i >= 1)
            def _():
                wait_scatter(i - 1)
            wait_scatter(i)


def _moe_plan(cls, n_tok, tb):
    n_tiles = -(-n_tok // tb) + N_CLASSES
    oh = (cls[:, None] == jnp.arange(N_CLASSES, dtype=I32)[None, :]).astype(I32)
    csum = jnp.cumsum(oh, axis=0)
    rank = jnp.take_along_axis(csum, cls[:, None], axis=1)[:, 0] - 1
    counts = csum[-1]
    tpc = (counts + tb - 1) // tb
    tend = jnp.cumsum(tpc)
    tstart = tend - tpc
    dest = tstart[cls] * tb + rank
    tile_cls = jnp.minimum(jnp.searchsorted(tend, jnp.arange(n_tiles, dtype=I32), side="right"),
                           N_CLASSES - 1).astype(I32)
    tile_ids = jnp.arange(n_tiles, dtype=I32)
    tcnt = jnp.clip(counts[tile_cls] - (tile_ids - tstart[tile_cls]) * tb, 0, tb)
    toks = jnp.arange(n_tok, dtype=I32)
    src = jnp.zeros((n_tiles * tb,), I32).at[dest].set(toks)
    idx = src.reshape(n_tiles, 1, tb)
    grp = tile_cls // len(PAIR_LO)
    pair = tile_cls % len(PAIR_LO)
    tlo = grp * EPG + jnp.asarray(PAIR_LO, I32)[pair]
    thi = grp * EPG + jnp.asarray(PAIR_HI, I32)[pair]
    return tend[-1:].astype(I32), tlo.astype(I32), thi.astype(I32), tcnt.astype(I32), idx, n_tiles


def _moe_call(h2ext, cls, w1, w3, w2, *, tb=MOE_TB):
    n_tok = h2ext.shape[0]
    nused, tlo, thi, tcnt, idx, n_tiles = _moe_plan(cls, n_tok, tb)
    wspec_lo = pl.BlockSpec((1, D, D), lambda i, nu, lo, hi, cnt: (lo[i], 0, 0))
    wspec_hi = pl.BlockSpec((1, D, D), lambda i, nu, lo, hi, cnt: (hi[i], 0, 0))
    grid_spec = pltpu.PrefetchScalarGridSpec(
        num_scalar_prefetch=4,
        grid=(n_tiles,),
        in_specs=[pl.BlockSpec(memory_space=pl.ANY), pl.BlockSpec(memory_space=pl.ANY),
                  wspec_lo, wspec_lo, wspec_lo, wspec_hi, wspec_hi, wspec_hi],
        out_specs=pl.BlockSpec(memory_space=pl.ANY),
        scratch_shapes=[
            pltpu.SMEM((3, 1, tb), I32),
            pltpu.VMEM((2, tb, D + META_W), F32),
            pltpu.VMEM((2, tb, D), F32),
            pltpu.SemaphoreType.DMA((3,)),
            pltpu.SemaphoreType.DMA((2,)),
            pltpu.SemaphoreType.DMA((2,)),
        ],
    )
    return pl.pallas_call(
        functools.partial(_moe_kernel, tb=tb),
        grid_spec=grid_spec,
        out_shape=jax.ShapeDtypeStruct((n_tok, D), F32),
        compiler_params=_cparams(("arbitrary",)),
        name="moe",
    )(nused, tlo, thi, tcnt, idx, h2ext, w1, w3, w2, w1, w3, w2)


def _m1_kernel(xp_ref, xc_ref, xn_ref, yp_ref, yc_ref, yn_ref, modp_ref, modc_ref, g_ref, win_ref, wconv_ref,
               bconv_ref, bdq_ref, bdk_ref, bdv_ref, wgT_ref, bg_ref,
               x2_ref, q_ref, k_ref, v_ref, o_ref, gT_ref, xm_ref):
    tm = xc_ref.shape[0]
    i = pl.program_id(1)
    nt = pl.num_programs(1)
    xe = jnp.concatenate([xp_ref[...], xc_ref[...], xn_ref[...]], axis=0)
    ye = jnp.concatenate([yp_ref[...], yc_ref[...], yn_ref[...]], axis=0)
    xe = xe + modp_ref[0, 5:6, :] * ye
    x2_ref[...] = xe[8:8 + tm]
    h = _rms_mod(xe, g_ref[...], modc_ref[0, 0:1, :], modc_ref[0, 1:2, :]).astype(BF16)
    xm = jnp.dot(h, win_ref[:, 0:E], preferred_element_type=F32)
    z = jnp.dot(h, win_ref[:, E:2 * E], preferred_element_type=F32)
    o_ref[...] = _sigmoid(z[8:8 + tm]).astype(BF16)
    rid = lax.broadcasted_iota(I32, (tm + 16, 1), 0)
    inside = jnp.logical_and(jnp.logical_or(rid >= 8, i > 0), jnp.logical_or(rid < tm + 8, i < nt - 1))
    xm = jnp.where(inside, xm, 0.0)
    xm_ref[...] = xm
    acc = jnp.zeros((tm, E), F32) + bconv_ref[...]
    for kk in range(SHORT_K):
        acc = acc + wconv_ref[kk:kk + 1, :] * xm_ref[6 + kk:6 + kk + tm, :]
    xcb = (acc * _sigmoid(acc)).astype(BF16)
    xmb = xm[8:8 + tm].astype(BF16)
    nt_dims = (((1,), (1,)), ((), ()))
    gT = jnp.zeros((4 * NH, tm), F32) + bg_ref[...]
    kscale = DH ** -0.5
    for j in range(E // MXU_DIM):
        sl = slice(j * MXU_DIM, (j + 1) * MXU_DIM)
        qj = jnp.dot(xcb[:, sl], bdq_ref[j], preferred_element_type=F32)
        kj = jnp.dot(xcb[:, sl], bdk_ref[j], preferred_element_type=F32)
        vj = jnp.dot(xmb[:, sl], bdv_ref[j], preferred_element_type=F32)
        qb, kb, vb = qj.astype(BF16), kj.astype(BF16), vj.astype(BF16)
        gT = gT + lax.dot_general(wgT_ref[:, sl], qb, nt_dims, preferred_element_type=F32)
        gT = gT + lax.dot_general(wgT_ref[:, E + j * MXU_DIM:E + (j + 1) * MXU_DIM], kb, nt_dims,
                                  preferred_element_type=F32)
        gT = gT + lax.dot_general(wgT_ref[:, 2 * E + j * MXU_DIM:2 * E + (j + 1) * MXU_DIM], vb, nt_dims,
                                  preferred_element_type=F32)
        q_ref[:, sl] = qb
        k_ref[:, sl] = (kj * kscale).astype(BF16)
        v_ref[:, sl] = vb
    gT_ref[...] = gT


def _m1_call(xa, ya, modp, modc, g, win, wconv, bconv, bdq, bdk, bdv, wgT, bg, *, nb, seq, tm, y_row_off):
    n = nb * seq
    tps = seq // tm
    hb = tm // 8
    nb8 = n // 8
    yo = y_row_off // tm
    yo8 = y_row_off // 8
    blk = lambda b, i: b * tps + i
    const = lambda shape: pl.BlockSpec(shape, lambda b, i: (0,) * len(shape))
    prev8 = lambda b, i: jnp.maximum(blk(b, i) * hb - 1, 0)
    next8 = lambda b, i: jnp.minimum((blk(b, i) + 1) * hb, nb8 - 1)
    in_specs = [
        pl.BlockSpec((8, D), lambda b, i: (prev8(b, i), 0)),
        pl.BlockSpec((tm, D), lambda b, i: (blk(b, i), 0)),
        pl.BlockSpec((8, D), lambda b, i: (next8(b, i), 0)),
        pl.BlockSpec((8, D), lambda b, i: (yo8 + prev8(b, i), 0)),
        pl.BlockSpec((tm, D), lambda b, i: (yo + blk(b, i), 0)),
        pl.BlockSpec((8, D), lambda b, i: (yo8 + next8(b, i), 0)),
        pl.BlockSpec((1, 8, D), lambda b, i: (b % modp.shape[0], 0, 0)),
        pl.BlockSpec((1, 8, D), lambda b, i: (b % modc.shape[0], 0, 0)),
        const((1, D)), const((D, 2 * E)), const((SHORT_K, E)), const((1, E)),
        const((E // MXU_DIM, MXU_DIM, MXU_DIM)), const((E // MXU_DIM, MXU_DIM, MXU_DIM)),
        const((E // MXU_DIM, MXU_DIM, MXU_DIM)), const((4 * NH, 3 * E)), const((4 * NH, 1)),
    ]
    tok = lambda w, dt: jax.ShapeDtypeStruct((n, w), dt)
    tspec = lambda w: pl.BlockSpec((tm, w), lambda b, i: (blk(b, i), 0))
    return pl.pallas_call(
        _m1_kernel,
        grid=(nb, tps),
        in_specs=in_specs,
        out_specs=[tspec(D), tspec(E), tspec(E), tspec(E), tspec(E),
                   pl.BlockSpec((4 * NH, tm), lambda b, i: (0, blk(b, i)))],
        out_shape=[tok(D, F32), tok(E, BF16), tok(E, BF16), tok(E, BF16), tok(E, BF16),
                   jax.ShapeDtypeStruct((4 * NH, n), F32)],
        scratch_shapes=[pltpu.VMEM((tm + 16, E), F32)],
        compiler_params=_cparams(("parallel", "arbitrary")),
        name="m1",
    )(xa, xa, xa, ya, ya, ya, modp, modc, g.reshape(1, D), win, wconv, bconv.reshape(1, E), bdq, bdk, bdv, wgT,
      bg.reshape(4 * NH, 1))


def _log_sigmoid(x):
    return jnp.minimum(x, 0.0) - jnp.log1p(jnp.exp(-jnp.abs(x)))


def _scan_step(q, k, v, i_row, f_row, c_ref, n_ref, m_ref, *, rev):
    t = q.shape[0]
    r = lax.broadcasted_iota(I32, (t, t), 0)
    c = lax.broadcasted_iota(I32, (t, t), 1)
    tri = (c >= r) if rev else (c <= r)
    tri_t = (r >= c) if rev else (r <= c)
    eye = c == r
    lf = _log_sigmoid(f_row)
    lf_b = jnp.broadcast_to(lf, (t, t))
    b_col = jnp.sum(jnp.where(tri, lf_b, 0.0), axis=1, keepdims=True)
    lf_col = jnp.sum(jnp.where(eye, lf_b, 0.0), axis=1, keepdims=True)
    i_col = jnp.sum(jnp.where(eye, jnp.broadcast_to(i_row, (t, t)), 0.0), axis=1, keepdims=True)
    b_row = jnp.sum(jnp.where(tri_t, jnp.broadcast_to(lf_col, (t, t)), 0.0), axis=0, keepdims=True)
    m = m_ref[0:1, 0:1]
    dmat = jnp.where(tri, b_col - b_row + i_row, -jnp.inf)
    inter = b_col + m
    m_t = jnp.maximum(inter, jnp.max(dmat, axis=1, keepdims=True))
    wmat = jnp.exp(dmat - m_t)
    w_inter = jnp.exp(inter - m_t)
    s = lax.dot_general(q, k, (((1,), (1,)), ((), ())), preferred_element_type=F32) * wmat
    cb = c_ref[...].astype(BF16)
    num = jnp.dot(s.astype(BF16), v, preferred_element_type=F32) \
        + w_inter * jnp.dot(q, cb, preferred_element_type=F32)
    qn = jnp.sum(q.astype(F32) * n_ref[...], axis=1, keepdims=True)
    den = jnp.sum(s, axis=1, keepdims=True) + w_inter * qn
    h = num / jnp.maximum(jnp.abs(den), jnp.exp(-m_t))
    b_end = jnp.sum(lf, axis=1, keepdims=True)
    dlog = b_end - b_col + i_col
    m_new = jnp.maximum(b_end + m, jnp.max(dlog, axis=0, keepdims=True))
    wk = jnp.exp(dlog - m_new)
    decay = jnp.exp(b_end + m - m_new)
    kf = k.astype(F32) * wk
    c_ref[...] = decay * c_ref[...] + lax.dot_general(kf.astype(BF16), v, (((0,), (0,)), ((), ())),
                                                      preferred_element_type=F32)
    n_ref[...] = decay * n_ref[...] + jnp.sum(kf, axis=0, keepdims=True)
    m_ref[...] = jnp.broadcast_to(m_new, m_ref.shape)
    return h


def _scan_kernel(*refs, rev, final):
    qc, kc, vc, ic, fc, ql, kl, vl, il, fl = refs[:10]
    if final:
        hf_ref, o_ref, hng_ref, out_ref, c_ref, n_ref, m_ref = refs[10:]
    else:
        out_ref, c_ref, n_ref, m_ref = refs[10:]
    j = pl.program_id(2)

    @pl.when(j == 0)
    def _():
        c_ref[...] = jnp.zeros(c_ref.shape, F32)
        n_ref[...] = jnp.zeros(n_ref.shape, F32)
        m_ref[...] = jnp.zeros(m_ref.shape, F32)
        _scan_step(qc[0], kc[0], vc[0], ic[0], fc[0], c_ref, n_ref, m_ref, rev=rev)

    @pl.when(j > 0)
    def _():
        h = _scan_step(ql[0], kl[0], vl[0], il[0], fl[0], c_ref, n_ref, m_ref, rev=rev)
        if final:
            hs = h + hf_ref[0]
            mu = jnp.mean(hs, axis=-1, keepdims=True)
            hc = hs - mu
            var = jnp.mean(hc * hc, axis=-1, keepdims=True)
            hn = hc * lax.rsqrt(var + EPS) * hng_ref[...]
            out_ref[0] = (o_ref[0] * hn.astype(BF16)).astype(BF16)
        else:
            out_ref[0] = h


def _scan_call(qc, kc, vc, gc, ql, kl, vl, gl, hf, o, hng, *, nb, seq, rev):
    final = rev
    t = CHUNK_T
    ncl = seq // t
    if rev:
        cidx = lambda j: ncl - jnp.maximum(j, 1)
        irow, frow = 2 * NH, 3 * NH
    else:
        cidx = lambda j: jnp.maximum(j - 1, 0)
        irow, frow = 0, NH
    cspec = pl.BlockSpec((1, t, DH), lambda b, h, j: (b, 0, h))
    lspec = pl.BlockSpec((1, t, DH), lambda b, h, j: (b, cidx(j), h))
    gcs = lambda r0: pl.BlockSpec((1, 1, t), lambda b, h, j: (r0 + h, 0, b))
    gls = lambda r0: pl.BlockSpec((1, 1, t), lambda b, h, j: (r0 + h, 0, b * ncl + cidx(j)))
    in_specs = [cspec, cspec, cspec, gcs(irow), gcs(frow), lspec, lspec, lspec, gls(irow), gls(frow)]
    args = [qc, kc, vc, gc, gc, ql, kl, vl, gl, gl]
    if final:
        in_specs += [lspec, lspec, pl.BlockSpec((1, DH), lambda b, h, j: (0, h))]
        args += [hf, o, hng.reshape(1, E)]
    return pl.pallas_call(
        functools.partial(_scan_kernel, rev=rev, final=final),
        grid=(nb, NH, ncl + 1),
        in_specs=in_specs,
        out_specs=lspec,
        out_shape=jax.ShapeDtypeStruct((nb, seq, E), BF16 if final else F32),
        scratch_shapes=[pltpu.VMEM((DH, DH), F32), pltpu.VMEM((1, DH), F32), pltpu.VMEM((8, LANES), F32)],
        compiler_params=_cparams(("parallel", "parallel", "arbitrary")),
        name="scan_bwd" if rev else "scan_fwd",
    )(*args)


def _m3_kernel(hg_ref, x_ref, mod_ref, wo_ref, gffn_ref, wrT_ref, br_ref, x1_ref, h2_ref, cls_ref, meta_ref):
    y = jnp.dot(hg_ref[...], wo_ref[...], preferred_element_type=F32)
    _tail(x_ref[...], y, mod_ref, gffn_ref, wrT_ref, br_ref, x1_ref, h2_ref, cls_ref, meta_ref)


def _m3_call(hg, x2d, mod, wo, gffn, wrT, br, *, seq, tm):
    n = x2d.shape[0]
    tps = seq // tm
    const = lambda shape: pl.BlockSpec(shape, lambda i: (0,) * len(shape))
    return pl.pallas_call(
        _m3_kernel,
        grid=(n // tm,),
        in_specs=[
            pl.BlockSpec((tm, E), lambda i: (i, 0)),
            pl.BlockSpec((tm, D), lambda i: (i, 0)),
            pl.BlockSpec((1, 8, D), lambda i: (i // tps, 0, 0)),
            const((E, D)), const((1, D)), const((N_EXPERTS, D)), const((N_EXPERTS, 1)),
        ],
        out_specs=[
            pl.BlockSpec((tm, D), lambda i: (i, 0)),
            pl.BlockSpec((tm, D + META_W), lambda i: (i, 0)),
            pl.BlockSpec((8, tm), lambda i: (0, i)),
        ],
        out_shape=[
            jax.ShapeDtypeStruct((n, D), F32),
            jax.ShapeDtypeStruct((n, D + META_W), F32),
            jax.ShapeDtypeStruct((8, n), I32),
        ],
        scratch_shapes=[pltpu.VMEM((META_W, tm), F32)],
        compiler_params=_cparams(("parallel",)),
        name="m3",
    )(hg, x2d, mod, wo, gffn.reshape(1, D), wrT, br.reshape(N_EXPERTS, 1))


def _final_kernel(x_ref, y_ref, mod_ref, g_ref, o_ref):
    x = x_ref[...] + mod_ref[0, 5:6, :] * y_ref[...]
    ms = jnp.mean(x * x, axis=-1, keepdims=True)
    o_ref[...] = x * lax.rsqrt(ms + EPS) * g_ref[...]


def _final_call(x2d, y, mod, g, *, seq, tm):
    n = x2d.shape[0]
    tps = seq // tm
    return pl.pallas_call(
        _final_kernel,
        grid=(n // tm,),
        in_specs=[
            pl.BlockSpec((tm, D), lambda i: (i, 0)),
            pl.BlockSpec((tm, D), lambda i: (i, 0)),
            pl.BlockSpec((1, 8, D), lambda i: (i // tps, 0, 0)),
            pl.BlockSpec((1, D), lambda i: (0, 0)),
        ],
        out_specs=pl.BlockSpec((tm, D), lambda i: (i, 0)),
        out_shape=jax.ShapeDtypeStruct((n, D), F32),
        compiler_params=_cparams(("parallel",)),
        name="final",
    )(x2d, y, mod, g.reshape(1, D))


def _blockdiag_tiles(w):
    per = MXU_DIM // 4
    w = w.reshape(E // MXU_DIM, per, 4, 4)
    eye = jnp.eye(per, dtype=w.dtype)
    t = w[:, :, :, None, :] * eye[None, :, None, :, None]
    return t.reshape(E // MXU_DIM, MXU_DIM, MXU_DIM)


def kernel(x, c, ctx, c_ctx, w_ada, b_ada, g_mix, g_ffn, g_final, w_in_ab, w_dw_a, b_dw_a, ln_g_a, ln_b_a, w_pool,
           b_pool, pool_scale, w_out_ab, w_in_c, w_conv_c, b_conv_c, w_q, w_k, w_v, w_gates, b_gates, hn_g, w_out_c,
           w_router, b_router, w1, w3, w2):
    nb, seq, _ = x.shape
    ctx_len = ctx.shape[1]
    assert ctx_len == CHUNK_T and seq % TM == 0 and seq % GRID_W == 0 and nb < 16
    n_lat = nb * seq
    n_ctx = nb * ctx_len
    x_lat = x.reshape(n_lat, D)
    x_ctx = ctx.reshape(n_ctx, D)

    s16 = jnp.zeros((16, D), F32).at[:nb].set(c).at[nb].set(c_ctx)
    mod = _mod_call(s16, w_ada, b_ada)
    pad8 = lambda m: jnp.pad(m.reshape(-1, 6, D), ((0, 0), (0, 2), (0, 0)))
    mod_lat = [pad8(mod[l, :nb]) for l in range(2)]
    mod_ctx = [pad8(mod[l, nb:nb + 1]) for l in range(2)]

    wrT = w_router.T
    bf = lambda w: w.astype(BF16)

    prm1 = (g_mix[0], bf(w_in_ab[0]), w_dw_a[0], b_dw_a[0], ln_g_a[0], ln_b_a[0])
    a_lat, bp_lat = _a1_call(x_lat, mod_lat[0], *prm1, seq=seq, tm=TM, seg=GRID_W)
    a_ctx, bp_ctx = _a1_call(x_ctx, mod_ctx[0], *prm1, seq=ctx_len, tm=ctx_len, seg=ctx_len)
    prm2 = (bf(w_pool[0]), b_pool[0], pool_scale[0], bf(w_out_ab[0]), g_ffn[0], wrT, b_router)
    x1_lat, h2_lat, cls_lat = _a2_call(bp_lat, a_lat, x_lat, mod_lat[0], *prm2, seq=seq, tm=TM, stride=GRID_W,
                                       npos=seq // GRID_W, use_halo=True)
    x1_ctx, h2_ctx, cls_ctx = _a2_call(bp_ctx, a_ctx, x_ctx, mod_ctx[0], *prm2, seq=ctx_len, tm=ctx_len,
                                       stride=1, npos=ctx_len, use_halo=False)
    h2ext = jnp.concatenate([h2_lat, h2_ctx], axis=0)
    cls0 = jnp.concatenate([cls_lat[0], cls_ctx[0]])
    y0 = _moe_call(h2ext, cls0, bf(w1[0]), bf(w3[0]), bf(w2[0]))

    wgT = bf(w_gates[0].T)
    prm3 = (g_mix[1], bf(w_in_c[0]), w_conv_c[0], b_conv_c[0], bf(_blockdiag_tiles(w_q[0])),
            bf(_blockdiag_tiles(w_k[0])), bf(_blockdiag_tiles(w_v[0])), wgT, b_gates[0])
    x2_lat, ql, kl, vl, ol, gl = _m1_call(x1_lat, y0, mod_lat[0], mod_lat[1], *prm3, nb=nb, seq=seq, tm=TM1,
                                          y_row_off=0)
    _, qc, kc, vc, _, gc = _m1_call(x1_ctx, y0, mod_ctx[0], mod_ctx[1], *prm3, nb=nb, seq=ctx_len, tm=ctx_len,
                                    y_row_off=n_lat)
    r3 = lambda t, length: t.reshape(nb, length, E)
    ql, kl, vl, ol = (r3(t, seq) for t in (ql, kl, vl, ol))
    qc, kc, vc = (r3(t, ctx_len) for t in (qc, kc, vc))
    gl = gl.reshape(4 * NH, 1, n_lat)
    gc = gc.reshape(4 * NH, 1, n_ctx)
    hf = _scan_call(qc, kc, vc, gc, ql, kl, vl, gl, None, None, None, nb=nb, seq=seq, rev=False)
    hg = _scan_call(qc, kc, vc, gc, ql, kl, vl, gl, hf, ol, hn_g[0], nb=nb, seq=seq, rev=True)
    x3_lat, h2ext1, cls1 = _m3_call(hg.reshape(n_lat, E), x2_lat, mod_lat[1], bf(w_out_c[0]), g_ffn[1], wrT,
                                    b_router, seq=seq, tm=TM)
    y1 = _moe_call(h2ext1, cls1[0], bf(w1[1]), bf(w3[1]), bf(w2[1]))
    out = _final_call(x3_lat, y1, mod_lat[1], g_final, seq=seq, tm=TM)
    return out.reshape(nb, seq, D)
```

```python
import functools

import jax
import jax.numpy as jnp
from jax import lax
from jax.experimental import pallas as pl
from jax.experimental.pallas import tpu as pltpu

F32 = jnp.float32
BF16 = jnp.bfloat16
I32 = jnp.int32
HIGHEST = lax.Precision.HIGHEST

D = 1024
EPS = 1e-6
GRID_W = 64
CONV_K = 31
POOL_WINDOWS = (2, 4, 8, 16)
POOL_GC = 256
E = 2 * D
NH = 4
DH = E // NH
SHORT_K = 4
N_EXPERTS = 16
EPG = 4
N_CLASSES = 24
PAIR_LO = (0, 0, 0, 1, 1, 2)
PAIR_HI = (1, 2, 3, 2, 3, 3)

LANES = 128
MXU_DIM = 256
TM = 512
TM1 = 256
CHUNK_T = 256
SCAN_HEADS = 4
MOE_TB = 256
META_W = LANES
VMEM_LIMIT = 56 * 1024 * 1024


def _cparams(sem):
    return pltpu.CompilerParams(dimension_semantics=sem, vmem_limit_bytes=VMEM_LIMIT)


def _sigmoid(x):
    return jax.nn.sigmoid(x)


def _rms_mod(x, g, shift, scale):
    ms = jnp.mean(x * x, axis=-1, keepdims=True)
    return (x * lax.rsqrt(ms + EPS) * g) * (1.0 + scale) + shift


def _mod_kernel(s_ref, w_ref, b_ref, o_ref):
    s = s_ref[...]
    s = s * _sigmoid(s)
    o_ref[0] = jnp.dot(s, w_ref[0], preferred_element_type=F32, precision=HIGHEST) + b_ref[0]


def _mod_call(s16, w_ada, b_ada):
    depth = w_ada.shape[0]
    tn = 1536
    return pl.pallas_call(
        _mod_kernel,
        grid=(depth, 6 * D // tn),
        in_specs=[
            pl.BlockSpec((16, D), lambda l, n: (0, 0)),
            pl.BlockSpec((1, D, tn), lambda l, n: (l, 0, n)),
            pl.BlockSpec((1, 1, tn), lambda l, n: (l, 0, n)),
        ],
        out_specs=pl.BlockSpec((1, 16, tn), lambda l, n: (l, 0, n)),
        out_shape=jax.ShapeDtypeStruct((depth, 16, 6 * D), F32),
        compiler_params=_cparams(("parallel", "parallel")),
        name="mod",
    )(s16, w_ada, b_ada.reshape(depth, 1, 6 * D))


def _a1_kernel(x_ref, mod_ref, g_ref, win_ref, wdw_ref, bdw_ref, lng_ref, lnb_ref,
               a_ref, bp_ref, pad_ref, conv_ref, shf_ref, *, seg):
    tm = x_ref.shape[0]
    nseg = tm // seg
    h = _rms_mod(x_ref[...], g_ref[...], mod_ref[0, 0:1, :], mod_ref[0, 1:2, :])
    u = jnp.dot(h.astype(BF16), win_ref[...], preferred_element_type=F32)
    a = u[:, :D] * _sigmoid(u[:, D:2 * D])
    bp_ref[...] = u[:, 2 * D:]
    zeros16 = jnp.zeros((nseg, 16, D), F32)
    pad_ref[:, 0:16, :] = zeros16
    pad_ref[:, seg + 16:seg + 32, :] = zeros16
    pad_ref[:, 16:16 + seg, :] = a.reshape(nseg, seg, D)

    def seg_body(s, carry):
        for cs in range(D // LANES):
            sl = slice(cs * LANES, (cs + 1) * LANES)
            acc = jnp.zeros((seg, LANES), F32)
            for r in range(8):
                starts = [st for st in range(r, CONV_K + 1, 8) if st >= 1]
                lo = starts[0]
                nrow = starts[-1] - lo + seg
                if r:
                    shf_ref[r, 0:nrow, :] = pad_ref[s, pl.ds(lo, nrow), sl]
                    src = lambda st: shf_ref[r, st - lo:st - lo + seg, :]
                else:
                    src = lambda st: pad_ref[s, pl.ds(st, seg), sl]
                for st in starts:
                    acc = acc + wdw_ref[st - 1:st, sl] * src(st)
            conv_ref[pl.ds(pl.multiple_of(s * seg, seg), seg), sl] = acc + bdw_ref[:, sl]
        return carry

    lax.fori_loop(0, nseg, seg_body, 0)
    cv = conv_ref[...]
    mu = jnp.mean(cv, axis=-1, keepdims=True)
    cc = cv - mu
    var = jnp.mean(cc * cc, axis=-1, keepdims=True)
    y = cc * lax.rsqrt(var + EPS) * lng_ref[...] + lnb_ref[...]
    a_ref[...] = (y * _sigmoid(y)).astype(BF16)


def _a1_call(x2d, mod, g, win, wdw, bdw, lng, lnb, *, seq, tm, seg):
    n = x2d.shape[0]
    tps = seq // tm
    row = lambda v: v.reshape(1, -1)
    const = lambda shape: pl.BlockSpec(shape, lambda i: (0,) * len(shape))
    return pl.pallas_call(
        functools.partial(_a1_kernel, seg=seg),
        grid=(n // tm,),
        in_specs=[
            pl.BlockSpec((tm, D), lambda i: (i, 0)),
            pl.BlockSpec((1, 8, D), lambda i: ((i // tps) % mod.shape[0], 0, 0)),
            const((1, D)), const((D, 3 * D)), const((CONV_K, D)), const((1, D)), const((1, D)), const((1, D)),
        ],
        out_specs=[pl.BlockSpec((tm, D), lambda i: (i, 0)), pl.BlockSpec((tm, D), lambda i: (i, 0))],
        out_shape=[jax.ShapeDtypeStruct((n, D), BF16), jax.ShapeDtypeStruct((n, D), F32)],
        scratch_shapes=[pltpu.VMEM((tm // seg, seg + 32, D), F32), pltpu.VMEM((tm, D), F32),
                        pltpu.VMEM((8, seg + 24, LANES), F32)],
        compiler_params=_cparams(("parallel",)),
        name="a1",
    )(x2d, mod, row(g), win, wdw, row(bdw), row(lng), row(lnb))


def _tail(x, y, mod_ref, gffn_ref, wrT_ref, br_ref, x1_ref, h2_ref, cls_ref, meta_ref):
    tm = x.shape[0]
    x1 = x + mod_ref[0, 2:3, :] * y
    x1_ref[...] = x1
    h2 = _rms_mod(x1, gffn_ref[...], mod_ref[0, 3:4, :], mod_ref[0, 4:5, :])
    logits = lax.dot_general(wrT_ref[...], h2, (((1,), (1,)), ((), ())),
                             precision=HIGHEST, preferred_element_type=F32)
    sc = _sigmoid(logits)
    bi = sc + br_ref[...]
    gs = []
    for g in range(N_EXPERTS // EPG):
        v = [bi[EPG * g + j:EPG * g + j + 1, :] for j in range(EPG)]
        best = v[0] + v[1]
        for (p, q) in ((0, 2), (0, 3), (1, 2), (1, 3), (2, 3)):
            best = jnp.maximum(best, v[p] + v[q])
        gs.append(best)
    gsel = jnp.zeros((1, tm), I32)
    gbest = gs[0]
    for g in range(1, N_EXPERTS // EPG):
        better = gs[g] > gbest
        gsel = jnp.where(better, g, gsel)
        gbest = jnp.where(better, gs[g], gbest)

    def pick(arr, j):
        out = arr[j:j + 1, :]
        for g in range(1, N_EXPERTS // EPG):
            out = jnp.where(gsel == g, arr[EPG * g + j:EPG * g + j + 1, :], out)
        return out

    cand = [pick(bi, j) for j in range(EPG)]
    scs = [pick(sc, j) for j in range(EPG)]
    i1 = jnp.zeros((1, tm), I32)
    b1 = cand[0]
    for j in range(1, EPG):
        better = cand[j] > b1
        i1 = jnp.where(better, j, i1)
        b1 = jnp.where(better, cand[j], b1)
    neg = jnp.full((1, tm), -jnp.inf, F32)
    i2 = jnp.zeros((1, tm), I32)
    b2 = neg
    for j in range(EPG):
        cj = jnp.where(i1 == j, neg, cand[j])
        better = cj > b2
        i2 = jnp.where(better, j, i2)
        b2 = jnp.where(better, cj, b2)
    w1 = jnp.zeros((1, tm), F32)
    w2 = jnp.zeros((1, tm), F32)
    for j in range(EPG):
        w1 = jnp.where(i1 == j, scs[j], w1)
        w2 = jnp.where(i2 == j, scs[j], w2)
    den = w1 + w2
    w1 = w1 / den
    w2 = w2 / den
    first_lo = i1 < i2
    lo = jnp.minimum(i1, i2)
    hi = jnp.maximum(i1, i2)
    w_lo = jnp.where(first_lo, w1, w2)
    w_hi = jnp.where(first_lo, w2, w1)
    pidx = jnp.where(lo == 0, hi - 1, jnp.where(lo == 1, hi + 1, 5))
    cls = gsel * len(PAIR_LO) + pidx
    cls_ref[...] = jnp.broadcast_to(cls, (8, tm))
    meta_ref[...] = jnp.zeros((META_W, tm), F32)
    meta_ref[0:1, :] = w_lo
    meta_ref[1:2, :] = w_hi
    h2_ref[:, 0:D] = h2
    h2_ref[:, D:D + META_W] = meta_ref[...].T


def _a2_kernel(bpp_ref, bpc_ref, bpn_ref, a_ref, x_ref, mod_ref, wpool_ref, bpool_ref, pscale_ref, wo_ref,
               gffn_ref, wrT_ref, br_ref, x1_ref, h2_ref, cls_ref, pb_ref, meta_ref, *, stride, npos, use_halo):
    tm = x_ref.shape[0]
    halo = 8 * stride
    i = pl.program_id(1)
    nt = pl.num_programs(1)
    cur = bpc_ref[...]
    if use_halo:
        pb_ref[0:halo, :] = jnp.where(i > 0, bpp_ref[...], 0.0)
        pb_ref[halo + tm:halo + tm + halo, :] = jnp.where(i < nt - 1, bpn_ref[...], 0.0)
    else:
        pb_ref[0:halo, :] = jnp.zeros((halo, D), F32)
        pb_ref[halo + tm:halo + tm + halo, :] = jnp.zeros((halo, D), F32)
    pb_ref[halo:halo + tm, :] = cur
    pos = (i * tm + lax.broadcasted_iota(I32, (tm, 1), 0)) // stride
    y = jnp.dot(a_ref[...], wo_ref[0:D, :], preferred_element_type=F32)
    for g, w in enumerate(POOL_WINDOWS):
        gsl = slice(g * POOL_GC, (g + 1) * POOL_GC)
        left = w // 2
        right = w - 1 - left
        tot = jnp.zeros((tm, POOL_GC), F32)
        for j in range(-left, right + 1):
            tot = tot + pb_ref[halo + j * stride:halo + j * stride + tm, gsl]
        cnt = (jnp.minimum(pos + right + 1, npos) - jnp.maximum(pos - left, 0)).astype(F32)
        pooled = tot / cnt - cur[:, gsl]
        pg = jnp.dot(pooled.astype(BF16), wpool_ref[g], preferred_element_type=F32) + bpool_ref[g:g + 1, :]
        pg = pg * pscale_ref[:, gsl]
        y = y + jnp.dot(pg.astype(BF16), wo_ref[D + g * POOL_GC:D + (g + 1) * POOL_GC, :],
                        preferred_element_type=F32)
    _tail(x_ref[...], y, mod_ref, gffn_ref, wrT_ref, br_ref, x1_ref, h2_ref, cls_ref, meta_ref)


def _a2_call(bp, a, x2d, mod, wpool, bpool, pscale, wo, gffn, wrT, br, *, seq, tm, stride, npos, use_halo):
    n = x2d.shape[0]
    nb = n // seq
    tps = seq // tm
    ntile = n // tm
    halo = 8 * stride
    blk = lambda b, i: b * tps + i
    row = lambda v: v.reshape(1, -1)
    const = lambda shape: pl.BlockSpec(shape, lambda b, i: (0,) * len(shape))
    in_specs = [
        pl.BlockSpec((tm, D), lambda b, i: (jnp.maximum(blk(b, i) - 1, 0), 0)),
        pl.BlockSpec((tm, D), lambda b, i: (blk(b, i), 0)),
        pl.BlockSpec((tm, D), lambda b, i: (jnp.minimum(blk(b, i) + 1, ntile - 1), 0)),
        pl.BlockSpec((tm, D), lambda b, i: (blk(b, i), 0)),
        pl.BlockSpec((tm, D), lambda b, i: (blk(b, i), 0)),
        pl.BlockSpec((1, 8, D), lambda b, i: (b % mod.shape[0], 0, 0)),
        const((4, POOL_GC, POOL_GC)), const((4, POOL_GC)), const((1, D)), const((2 * D, D)),
        const((1, D)), const((N_EXPERTS, D)), const((N_EXPERTS, 1)),
    ]
    args = [bp, bp, bp, a, x2d, mod, wpool, bpool, row(pscale), wo, row(gffn), wrT, br.reshape(N_EXPERTS, 1)]
    return pl.pallas_call(
        functools.partial(_a2_kernel, stride=stride, npos=npos, use_halo=use_halo),
        grid=(nb, tps),
        in_specs=in_specs,
        out_specs=[
            pl.BlockSpec((tm, D), lambda b, i: (blk(b, i), 0)),
            pl.BlockSpec((tm, D + META_W), lambda b, i: (blk(b, i), 0)),
            pl.BlockSpec((8, tm), lambda b, i: (0, blk(b, i))),
        ],
        out_shape=[
            jax.ShapeDtypeStruct((n, D), F32),
            jax.ShapeDtypeStruct((n, D + META_W), F32),
            jax.ShapeDtypeStruct((8, n), I32),
        ],
        scratch_shapes=[pltpu.VMEM((tm + 2 * halo, D), F32), pltpu.VMEM((META_W, tm), F32)],
        compiler_params=_cparams(("parallel", "arbitrary")),
        name="a2",
    )(*args)


def _moe_kernel(tlo_ref, thi_ref, tcnt_ref, idx_hbm, h2_hbm, w1lo, w3lo, w2lo, w1hi, w3hi, w2hi, y_hbm,
                idx_smem, xin, xb_ref, wt_ref, ybuf, sem_idx, sem_g, sem_s, *, tb, n_tiles):
    i = pl.program_id(0)
    last = n_tiles - 1
    unroll = 8

    def idx_copy(t):
        slot = lax.rem(t, 4)
        return pltpu.make_async_copy(idx_hbm.at[t], idx_smem.at[slot], sem_idx.at[slot])

    def gather_copy(tok, j):
        return pltpu.make_async_copy(h2_hbm.at[pl.ds(tok, 1), :], xin.at[pl.ds(j, 1), :], sem_g)

    def gather_all():
        return pltpu.make_async_copy(h2_hbm.at[pl.ds(0, tb), :], xin, sem_g)

    def scatter_copy(dst, slot2, j):
        return pltpu.make_async_copy(ybuf.at[slot2, pl.ds(j, 1), :], y_hbm.at[pl.ds(dst, 1), :], sem_s.at[slot2])

    def rows(fn, count=tb):
        def body(jj, c):
            for u in range(unroll):
                fn(jj * unroll + u)
            return c
        full = count // unroll
        lax.fori_loop(0, full, body, 0)
        if not isinstance(count, int):
            def tail(j, c):
                fn(j)
                return c
            lax.fori_loop(full * unroll, count, tail, 0)

    def wait_scatter(t):
        s2 = lax.rem(t, 2)
        rows(lambda j: scatter_copy(0, s2, j).wait(), tcnt_ref[t])

    @pl.when(i == 0)
    def _():
        idx_copy(0).start()
        idx_copy(0).wait()
        rows(lambda j: gather_copy(idx_smem[0, 0, j], j).start())
        idx_copy(1).start()

    @pl.when(i < last)
    def _():
        idx_copy(i + 1).wait()

    @pl.when(i + 2 <= last)
    def _():
        idx_copy(i + 2).start()

    gather_all().wait()

    @pl.when(i >= 2)
    def _():
        wait_scatter(i - 2)

    xb_ref[...] = xin[:, 0:D].astype(BF16)
    wt_ref[...] = xin[:, D:D + META_W]
    s4n = lax.rem(jnp.minimum(i + 1, last), 4)
    s2 = lax.rem(i, 2)
    tp = jnp.maximum(i - 1, 0)
    cntp = jnp.where(i >= 1, tcnt_ref[tp], 0)
    s4p = lax.rem(tp, 4)
    for j in range(tb):
        gather_copy(idx_smem[s4n, 0, j], j).start()
    for j in range(tb):
        @pl.when(j < cntp)
        def _():
            scatter_copy(idx_smem[s4p, 0, j], 1 - s2, j).start()
    xb = xb_ref[...]

    def ffn(w1, w3, w2):
        h1 = jnp.dot(xb, w1[0], preferred_element_type=F32)
        h3 = jnp.dot(xb, w3[0], preferred_element_type=F32)
        act = (h1 * _sigmoid(h1)) * h3
        return jnp.dot(act.astype(BF16), w2[0], preferred_element_type=F32)

    y = wt_ref[:, 0:1] * ffn(w1lo, w3lo, w2lo) + wt_ref[:, 1:2] * ffn(w1hi, w3hi, w2hi)
    ybuf[s2] = y

    @pl.when(i == last)
    def _():
        wait_scatter(i - 1)
        gather_all().wait()


def _moe_plan(cls, n_tok, tb):
    n_tiles = -(-n_tok // tb) + N_CLASSES
    oh = (cls[:, None] == jnp.arange(N_CLASSES, dtype=I32)[None, :]).astype(I32)
    csum = jnp.cumsum(oh, axis=0)
    rank = jnp.take_along_axis(csum, cls[:, None], axis=1)[:, 0] - 1
    counts = csum[-1]
    tpc = (counts + tb - 1) // tb
    tend = jnp.cumsum(tpc)
    tstart = tend - tpc
    dest = tstart[cls] * tb + rank
    tile_cls = jnp.minimum(jnp.searchsorted(tend, jnp.arange(n_tiles, dtype=I32), side="right"),
                           N_CLASSES - 1).astype(I32)
    tile_ids = jnp.arange(n_tiles, dtype=I32)
    tcnt = jnp.clip(counts[tile_cls] - (tile_ids - tstart[tile_cls]) * tb, 0, tb)
    toks = jnp.arange(n_tok, dtype=I32)
    src = jnp.zeros((n_tiles * tb,), I32).at[dest].set(toks)
    idx = src.reshape(n_tiles, 1, tb)
    grp = tile_cls // len(PAIR_LO)
    pair = tile_cls % len(PAIR_LO)
    tlo = grp * EPG + jnp.asarray(PAIR_LO, I32)[pair]
    thi = grp * EPG + jnp.asarray(PAIR_HI, I32)[pair]
    return tlo.astype(I32), thi.astype(I32), tcnt.astype(I32), idx, n_tiles


def _moe_call(h2ext, cls, w1, w3, w2, *, tb=MOE_TB):
    n_tok = h2ext.shape[0]
    assert n_tok % tb == 0
    tlo, thi, tcnt, idx, n_tiles = _moe_plan(cls, n_tok, tb)
    wspec_lo = pl.BlockSpec((1, D, D), lambda i, lo, hi, cnt: (lo[i], 0, 0))
    wspec_hi = pl.BlockSpec((1, D, D), lambda i, lo, hi, cnt: (hi[i], 0, 0))
    grid_spec = pltpu.PrefetchScalarGridSpec(
        num_scalar_prefetch=3,
        grid=(n_tiles,),
        in_specs=[pl.BlockSpec(memory_space=pl.ANY), pl.BlockSpec(memory_space=pl.ANY),
                  wspec_lo, wspec_lo, wspec_lo, wspec_hi, wspec_hi, wspec_hi],
        out_specs=pl.BlockSpec(memory_space=pl.ANY),
        scratch_shapes=[
            pltpu.SMEM((4, 1, tb), I32),
            pltpu.VMEM((tb, D + META_W), F32),
            pltpu.VMEM((tb, D), BF16),
            pltpu.VMEM((tb, META_W), F32),
            pltpu.VMEM((2, tb, D), F32),
            pltpu.SemaphoreType.DMA((4,)),
            pltpu.SemaphoreType.DMA(()),
            pltpu.SemaphoreType.DMA((2,)),
        ],
    )
    return pl.pallas_call(
        functools.partial(_moe_kernel, tb=tb, n_tiles=n_tiles),
        grid_spec=grid_spec,
        out_shape=jax.ShapeDtypeStruct((n_tok, D), F32),
        compiler_params=_cparams(("arbitrary",)),
        name="moe",
    )(tlo, thi, tcnt, idx, h2ext, w1, w3, w2, w1, w3, w2)


def _m1_kernel(xp_ref, xc_ref, xn_ref, yp_ref, yc_ref, yn_ref, modp_ref, modc_ref, g_ref, win_ref, wconv_ref,
               bconv_ref, bdq_ref, bdk_ref, bdv_ref, wgT_ref, bg_ref,
               x2_ref, q_ref, k_ref, v_ref, o_ref, gT_ref, xm_ref):
    tm = xc_ref.shape[0]
    i = pl.program_id(1)
    nt = pl.num_programs(1)
    xe = jnp.concatenate([xp_ref[...], xc_ref[...], xn_ref[...]], axis=0)
    ye = jnp.concatenate([yp_ref[...], yc_ref[...], yn_ref[...]], axis=0)
    xe = xe + modp_ref[0, 5:6, :] * ye
    x2_ref[...] = xe[8:8 + tm]
    h = _rms_mod(xe, g_ref[...], modc_ref[0, 0:1, :], modc_ref[0, 1:2, :]).astype(BF16)
    xm = jnp.dot(h, win_ref[:, 0:E], preferred_element_type=F32)
    z = jnp.dot(h, win_ref[:, E:2 * E], preferred_element_type=F32)
    o_ref[...] = _sigmoid(z[8:8 + tm]).astype(BF16)
    rid = lax.broadcasted_iota(I32, (tm + 16, 1), 0)
    inside = jnp.logical_and(jnp.logical_or(rid >= 8, i > 0), jnp.logical_or(rid < tm + 8, i < nt - 1))
    xm = jnp.where(inside, xm, 0.0)
    xm_ref[...] = xm
    acc = jnp.zeros((tm, E), F32) + bconv_ref[...]
    for kk in range(SHORT_K):
        acc = acc + wconv_ref[kk:kk + 1, :] * xm_ref[6 + kk:6 + kk + tm, :]
    xcb = (acc * _sigmoid(acc)).astype(BF16)
    xmb = xm[8:8 + tm].astype(BF16)
    nt_dims = (((1,), (1,)), ((), ()))
    gT = jnp.zeros((4 * NH, tm), F32) + bg_ref[...]
    kscale = DH ** -0.5
    for j in range(E // MXU_DIM):
        sl = slice(j * MXU_DIM, (j + 1) * MXU_DIM)
        qj = jnp.dot(xcb[:, sl], bdq_ref[j], preferred_element_type=F32)
        kj = jnp.dot(xcb[:, sl], bdk_ref[j], preferred_element_type=F32)
        vj = jnp.dot(xmb[:, sl], bdv_ref[j], preferred_element_type=F32)
        qb, kb, vb = qj.astype(BF16), kj.astype(BF16), vj.astype(BF16)
        gT = gT + lax.dot_general(wgT_ref[:, sl], qb, nt_dims, preferred_element_type=F32)
        gT = gT + lax.dot_general(wgT_ref[:, E + j * MXU_DIM:E + (j + 1) * MXU_DIM], kb, nt_dims,
                                  preferred_element_type=F32)
        gT = gT + lax.dot_general(wgT_ref[:, 2 * E + j * MXU_DIM:2 * E + (j + 1) * MXU_DIM], vb, nt_dims,
                                  preferred_element_type=F32)
        q_ref[:, sl] = qb
        k_ref[:, sl] = (kj * kscale).astype(BF16)
        v_ref[:, sl] = vb
    gT_ref[...] = gT


def _m1_call(xa, ya, modp, modc, g, win, wconv, bconv, bdq, bdk, bdv, wgT, bg, *, nb, seq, tm, y_row_off):
    n = nb * seq
    tps = seq // tm
    hb = tm // 8
    nb8 = n // 8
    yo = y_row_off // tm
    yo8 = y_row_off // 8
    blk = lambda b, i: b * tps + i
    const = lambda shape: pl.BlockSpec(shape, lambda b, i: (0,) * len(shape))
    prev8 = lambda b, i: jnp.maximum(blk(b, i) * hb - 1, 0)
    next8 = lambda b, i: jnp.minimum((blk(b, i) + 1) * hb, nb8 - 1)
    in_specs = [
        pl.BlockSpec((8, D), lambda b, i: (prev8(b, i), 0)),
        pl.BlockSpec((tm, D), lambda b, i: (blk(b, i), 0)),
        pl.BlockSpec((8, D), lambda b, i: (next8(b, i), 0)),
        pl.BlockSpec((8, D), lambda b, i: (yo8 + prev8(b, i), 0)),
        pl.BlockSpec((tm, D), lambda b, i: (yo + blk(b, i), 0)),
        pl.BlockSpec((8, D), lambda b, i: (yo8 + next8(b, i), 0)),
        pl.BlockSpec((1, 8, D), lambda b, i: (b % modp.shape[0], 0, 0)),
        pl.BlockSpec((1, 8, D), lambda b, i: (b % modc.shape[0], 0, 0)),
        const((1, D)), const((D, 2 * E)), const((SHORT_K, E)), const((1, E)),
        const((E // MXU_DIM, MXU_DIM, MXU_DIM)), const((E // MXU_DIM, MXU_DIM, MXU_DIM)),
        const((E // MXU_DIM, MXU_DIM, MXU_DIM)), const((4 * NH, 3 * E)), const((4 * NH, 1)),
    ]
    tok = lambda w, dt: jax.ShapeDtypeStruct((n, w), dt)
    tspec = lambda w: pl.BlockSpec((tm, w), lambda b, i: (blk(b, i), 0))
    return pl.pallas_call(
        _m1_kernel,
        grid=(nb, tps),
        in_specs=in_specs,
        out_specs=[tspec(D), tspec(E), tspec(E), tspec(E), tspec(E),
                   pl.BlockSpec((4 * NH, tm), lambda b, i: (0, blk(b, i)))],
        out_shape=[tok(D, F32), tok(E, BF16), tok(E, BF16), tok(E, BF16), tok(E, BF16),
                   jax.ShapeDtypeStruct((4 * NH, n), F32)],
        scratch_shapes=[pltpu.VMEM((tm + 16, E), F32)],
        compiler_params=_cparams(("parallel", "arbitrary")),
        name="m1",
    )(xa, xa, xa, ya, ya, ya, modp, modc, g.reshape(1, D), win, wconv, bconv.reshape(1, E), bdq, bdk, bdv, wgT,
      bg.reshape(4 * NH, 1))


def _log_sigmoid(x):
    return jnp.minimum(x, 0.0) - jnp.log1p(jnp.exp(-jnp.abs(x)))


def _scan_masks(t, rev):
    r = lax.broadcasted_iota(I32, (t, t), 0)
    c = lax.broadcasted_iota(I32, (t, t), 1)
    tri = (c >= r) if rev else (c <= r)
    tri_t = (r >= c) if rev else (r <= c)
    return tri, tri_t, c == r


def _scan_step(q, k, v, i_row, f_row, c_ref, n_ref, m_ref, masks):
    t = q.shape[0]
    tri, tri_t, eye = masks
    lf = _log_sigmoid(f_row)
    lf_b = jnp.broadcast_to(lf, (t, t))
    b_col = jnp.sum(jnp.where(tri, lf_b, 0.0), axis=1, keepdims=True)
    lf_col = jnp.sum(jnp.where(eye, lf_b, 0.0), axis=1, keepdims=True)
    i_col = jnp.sum(jnp.where(eye, jnp.broadcast_to(i_row, (t, t)), 0.0), axis=1, keepdims=True)
    b_row = jnp.sum(jnp.where(tri_t, jnp.broadcast_to(lf_col, (t, t)), 0.0), axis=0, keepdims=True)
    m = m_ref[0:1, 0:1]
    dmat = jnp.where(tri, b_col - b_row + i_row, -jnp.inf)
    inter = b_col + m
    m_t = jnp.maximum(inter, jnp.max(dmat, axis=1, keepdims=True))
    wmat = jnp.exp(dmat - m_t)
    w_inter = jnp.exp(inter - m_t)
    s = lax.dot_general(q, k, (((1,), (1,)), ((), ())), preferred_element_type=F32) * wmat
    cb = c_ref[...].astype(BF16)
    num = jnp.dot(s.astype(BF16), v, preferred_element_type=F32) \
        + w_inter * jnp.dot(q, cb, preferred_element_type=F32)
    qn = jnp.sum(q.astype(F32) * n_ref[...], axis=1, keepdims=True)
    den = jnp.sum(s, axis=1, keepdims=True) + w_inter * qn
    h = num / jnp.maximum(jnp.abs(den), jnp.exp(-m_t))
    b_end = jnp.sum(lf, axis=1, keepdims=True)
    dlog = b_end - b_col + i_col
    m_new = jnp.maximum(b_end + m, jnp.max(dlog, axis=0, keepdims=True))
    wk = jnp.exp(dlog - m_new)
    decay = jnp.exp(b_end + m - m_new)
    kf = k.astype(F32) * wk
    c_ref[...] = decay * c_ref[...] + lax.dot_general(kf.astype(BF16), v, (((0,), (0,)), ((), ())),
                                                      preferred_element_type=F32)
    n_ref[...] = decay * n_ref[...] + jnp.sum(kf, axis=0, keepdims=True)
    m_ref[...] = jnp.broadcast_to(m_new, m_ref.shape)
    return h


def _scan_kernel(*refs, rev, final):
    qc, kc, vc, ic, fc, ql, kl, vl, il, fl = refs[:10]
    if final:
        hf_ref, o_ref, hng_ref, out_ref, c_ref, n_ref, m_ref = refs[10:]
    else:
        out_ref, c_ref, n_ref, m_ref = refs[10:]
    j = pl.program_id(2)
    masks = _scan_masks(CHUNK_T, rev)
    heads = [slice(hh * DH, (hh + 1) * DH) for hh in range(SCAN_HEADS)]

    def step(qr, kr, vr, ir, fr, hh):
        hs = heads[hh]
        return _scan_step(qr[0, :, hs], kr[0, :, hs], vr[0, :, hs], ir[hh], fr[hh],
                          c_ref.at[hh], n_ref.at[hh], m_ref.at[hh], masks)

    @pl.when(j == 0)
    def _():
        c_ref[...] = jnp.zeros(c_ref.shape, F32)
        n_ref[...] = jnp.zeros(n_ref.shape, F32)
        m_ref[...] = jnp.zeros(m_ref.shape, F32)
        for hh in range(SCAN_HEADS):
            step(qc, kc, vc, ic, fc, hh)

    @pl.when(j > 0)
    def _():
        for hh in range(SCAN_HEADS):
            h = step(ql, kl, vl, il, fl, hh)
            if final:
                hs = h + hf_ref[0, :, heads[hh]]
                mu = jnp.mean(hs, axis=-1, keepdims=True)
                hc = hs - mu
                var = jnp.mean(hc * hc, axis=-1, keepdims=True)
                hn = hc * lax.rsqrt(var + EPS) * hng_ref[:, heads[hh]]
                out_ref[0, :, heads[hh]] = (o_ref[0, :, heads[hh]] * hn.astype(BF16)).astype(BF16)
            else:
                out_ref[0, :, heads[hh]] = h


def _scan_call(qc, kc, vc, gc, ql, kl, vl, gl, hf, o, hng, *, nb, seq, rev):
    final = rev
    t = CHUNK_T
    ncl = seq // t
    if rev:
        cidx = lambda j: ncl - jnp.maximum(j, 1)
        irow, frow = 2 * NH, 3 * NH
    else:
        cidx = lambda j: jnp.maximum(j - 1, 0)
        irow, frow = 0, NH
    hp = SCAN_HEADS
    wid = hp * DH
    cspec = pl.BlockSpec((1, t, wid), lambda b, h, j: (b, 0, h))
    lspec = pl.BlockSpec((1, t, wid), lambda b, h, j: (b, cidx(j), h))
    gcs = lambda r0: pl.BlockSpec((hp, 1, t), lambda b, h, j: (r0 // hp + h, 0, b))
    gls = lambda r0: pl.BlockSpec((hp, 1, t), lambda b, h, j: (r0 // hp + h, 0, b * ncl + cidx(j)))
    in_specs = [cspec, cspec, cspec, gcs(irow), gcs(frow), lspec, lspec, lspec, gls(irow), gls(frow)]
    args = [qc, kc, vc, gc, gc, ql, kl, vl, gl, gl]
    if final:
        in_specs += [lspec, lspec, pl.BlockSpec((1, wid), lambda b, h, j: (0, h))]
        args += [hf, o, hng.reshape(1, E)]
    return pl.pallas_call(
        functools.partial(_scan_kernel, rev=rev, final=final),
        grid=(nb, NH // hp, ncl + 1),
        in_specs=in_specs,
        out_specs=lspec,
        out_shape=jax.ShapeDtypeStruct((nb, seq, E), BF16 if final else F32),
        scratch_shapes=[pltpu.VMEM((hp, DH, DH), F32), pltpu.VMEM((hp, 1, DH), F32),
                        pltpu.VMEM((hp, 8, LANES), F32)],
        compiler_params=_cparams(("parallel", "parallel", "arbitrary")),
        name="scan_bwd" if rev else "scan_fwd",
    )(*args)


def _m3_kernel(hg_ref, x_ref, mod_ref, wo_ref, gffn_ref, wrT_ref, br_ref, x1_ref, h2_ref, cls_ref, meta_ref):
    y = jnp.dot(hg_ref[...], wo_ref[...], preferred_element_type=F32)
    _tail(x_ref[...], y, mod_ref, gffn_ref, wrT_ref, br_ref, x1_ref, h2_ref, cls_ref, meta_ref)


def _m3_call(hg, x2d, mod, wo, gffn, wrT, br, *, seq, tm):
    n = x2d.shape[0]
    tps = seq // tm
    const = lambda shape: pl.BlockSpec(shape, lambda i: (0,) * len(shape))
    return pl.pallas_call(
        _m3_kernel,
        grid=(n // tm,),
        in_specs=[
            pl.BlockSpec((tm, E), lambda i: (i, 0)),
            pl.BlockSpec((tm, D), lambda i: (i, 0)),
            pl.BlockSpec((1, 8, D), lambda i: (i // tps, 0, 0)),
            const((E, D)), const((1, D)), const((N_EXPERTS, D)), const((N_EXPERTS, 1)),
        ],
        out_specs=[
            pl.BlockSpec((tm, D), lambda i: (i, 0)),
            pl.BlockSpec((tm, D + META_W), lambda i: (i, 0)),
            pl.BlockSpec((8, tm), lambda i: (0, i)),
        ],
        out_shape=[
            jax.ShapeDtypeStruct((n, D), F32),
            jax.ShapeDtypeStruct((n, D + META_W), F32),
            jax.ShapeDtypeStruct((8, n), I32),
        ],
        scratch_shapes=[pltpu.VMEM((META_W, tm), F32)],
        compiler_params=_cparams(("parallel",)),
        name="m3",
    )(hg, x2d, mod, wo, gffn.reshape(1, D), wrT, br.reshape(N_EXPERTS, 1))


def _final_kernel(x_ref, y_ref, mod_ref, g_ref, o_ref):
    x = x_ref[...] + mod_ref[0, 5:6, :] * y_ref[...]
    ms = jnp.mean(x * x, axis=-1, keepdims=True)
    o_ref[...] = x * lax.rsqrt(ms + EPS) * g_ref[...]


def _final_call(x2d, y, mod, g, *, seq, tm):
    n = x2d.shape[0]
    tps = seq // tm
    return pl.pallas_call(
        _final_kernel,
        grid=(n // tm,),
        in_specs=[
            pl.BlockSpec((tm, D), lambda i: (i, 0)),
            pl.BlockSpec((tm, D), lambda i: (i, 0)),
            pl.BlockSpec((1, 8, D), lambda i: (i // tps, 0, 0)),
            pl.BlockSpec((1, D), lambda i: (0, 0)),
        ],
        out_specs=pl.BlockSpec((tm, D), lambda i: (i, 0)),
        out_shape=jax.ShapeDtypeStruct((n, D), F32),
        compiler_params=_cparams(("parallel",)),
        name="final",
    )(x2d, y, mod, g.reshape(1, D))


def _blockdiag_tiles(w):
    per = MXU_DIM // 4
    w = w.reshape(E // MXU_DIM, per, 4, 4)
    eye = jnp.eye(per, dtype=w.dtype)
    t = w[:, :, :, None, :] * eye[None, :, None, :, None]
    return t.reshape(E // MXU_DIM, MXU_DIM, MXU_DIM)


def kernel(x, c, ctx, c_ctx, w_ada, b_ada, g_mix, g_ffn, g_final, w_in_ab, w_dw_a, b_dw_a, ln_g_a, ln_b_a, w_pool,
           b_pool, pool_scale, w_out_ab, w_in_c, w_conv_c, b_conv_c, w_q, w_k, w_v, w_gates, b_gates, hn_g, w_out_c,
           w_router, b_router, w1, w3, w2):
    nb, seq, _ = x.shape
    ctx_len = ctx.shape[1]
    assert ctx_len == CHUNK_T and seq % TM == 0 and seq % GRID_W == 0 and nb < 16
    n_lat = nb * seq
    n_ctx = nb * ctx_len
    x_lat = x.reshape(n_lat, D)
    x_ctx = ctx.reshape(n_ctx, D)

    s16 = jnp.zeros((16, D), F32).at[:nb].set(c).at[nb].set(c_ctx)
    mod = _mod_call(s16, w_ada, b_ada)
    pad8 = lambda m: jnp.pad(m.reshape(-1, 6, D), ((0, 0), (0, 2), (0, 0)))
    mod_lat = [pad8(mod[l, :nb]) for l in range(2)]
    mod_ctx = [pad8(mod[l, nb:nb + 1]) for l in range(2)]

    wrT = w_router.T
    bf = lambda w: w.astype(BF16)

    prm1 = (g_mix[0], bf(w_in_ab[0]), w_dw_a[0], b_dw_a[0], ln_g_a[0], ln_b_a[0])
    a_lat, bp_lat = _a1_call(x_lat, mod_lat[0], *prm1, seq=seq, tm=TM, seg=GRID_W)
    a_ctx, bp_ctx = _a1_call(x_ctx, mod_ctx[0], *prm1, seq=ctx_len, tm=ctx_len, seg=ctx_len)
    prm2 = (bf(w_pool[0]), b_pool[0], pool_scale[0], bf(w_out_ab[0]), g_ffn[0], wrT, b_router)
    x1_lat, h2_lat, cls_lat = _a2_call(bp_lat, a_lat, x_lat, mod_lat[0], *prm2, seq=seq, tm=TM, stride=GRID_W,
                                       npos=seq // GRID_W, use_halo=True)
    x1_ctx, h2_ctx, cls_ctx = _a2_call(bp_ctx, a_ctx, x_ctx, mod_ctx[0], *prm2, seq=ctx_len, tm=ctx_len,
                                       stride=1, npos=ctx_len, use_halo=False)
    h2ext = jnp.concatenate([h2_lat, h2_ctx], axis=0)
    cls0 = jnp.concatenate([cls_lat[0], cls_ctx[0]])
    y0 = _moe_call(h2ext, cls0, bf(w1[0]), bf(w3[0]), bf(w2[0]))

    wgT = bf(w_gates[0].T)
    prm3 = (g_mix[1], bf(w_in_c[0]), w_conv_c[0], b_conv_c[0], bf(_blockdiag_tiles(w_q[0])),
            bf(_blockdiag_tiles(w_k[0])), bf(_blockdiag_tiles(w_v[0])), wgT, b_gates[0])
    x2_lat, ql, kl, vl, ol, gl = _m1_call(x1_lat, y0, mod_lat[0], mod_lat[1], *prm3, nb=nb, seq=seq, tm=TM1,
                                          y_row_off=0)
    _, qc, kc, vc, _, gc = _m1_call(x1_ctx, y0, mod_ctx[0], mod_ctx[1], *prm3, nb=nb, seq=ctx_len, tm=ctx_len,
                                    y_row_off=n_lat)
    r3 = lambda t, length: t.reshape(nb, length, E)
    ql, kl, vl, ol = (r3(t, seq) for t in (ql, kl, vl, ol))
    qc, kc, vc = (r3(t, ctx_len) for t in (qc, kc, vc))
    gl = gl.reshape(4 * NH, 1, n_lat)
    gc = gc.reshape(4 * NH, 1, n_ctx)
    hf = _scan_call(qc, kc, vc, gc, ql, kl, vl, gl, None, None, None, nb=nb, seq=seq, rev=False)
    hg = _scan_call(qc, kc, vc, gc, ql, kl, vl, gl, hf, ol, hn_g[0], nb=nb, seq=seq, rev=True)
    x3_lat, h2ext1, cls1 = _m3_call(hg.reshape(n_lat, E), x2_lat, mod_lat[1], bf(w_out_c[0]), g_ffn[1], wrT,
                                    b_router, seq=seq, tm=TM)
    y1 = _moe_call(h2ext1, cls1[0], bf(w1[1]), bf(w3[1]), bf(w2[1]))
    out = _final_call(x3_lat, y1, mod_lat[1], g_final, seq=seq, tm=TM)
    return out.reshape(nb, seq, D)
```

```python
import functools

import jax
import jax.numpy as jnp
from jax import lax
from jax.experimental import pallas as pl
from jax.experimental.pallas import tpu as pltpu

F32 = jnp.float32
BF16 = jnp.bfloat16
I32 = jnp.int32
HIGHEST = lax.Precision.HIGHEST

D = 1024
EPS = 1e-6
GRID_W = 64
CONV_K = 31
POOL_WINDOWS = (2, 4, 8, 16)
POOL_GC = 256
E = 2 * D
NH = 4
DH = E // NH
SHORT_K = 4
N_EXPERTS = 16
EPG = 4
N_CLASSES = 24
PAIR_LO = (0, 0, 0, 1, 1, 2)
PAIR_HI = (1, 2, 3, 2, 3, 3)

LANES = 128
MXU_DIM = 256
TM = 512
TM1 = 256
CHUNK_T = 256
SCAN_HEADS = 4
MOE_TB = 256
VMEM_LIMIT = 56 * 1024 * 1024


def _cparams(sem):
    return pltpu.CompilerParams(dimension_semantics=sem, vmem_limit_bytes=VMEM_LIMIT)


def _sigmoid(x):
    return jax.nn.sigmoid(x)


def _rms_mod(x, g, shift, scale):
    ms = jnp.mean(x * x, axis=-1, keepdims=True)
    return (x * lax.rsqrt(ms + EPS) * g) * (1.0 + scale) + shift


def _mod_kernel(s_ref, w_ref, b_ref, o_ref):
    s = s_ref[...]
    s = s * _sigmoid(s)
    o_ref[0] = jnp.dot(s, w_ref[0], preferred_element_type=F32, precision=HIGHEST) + b_ref[0]


def _mod_call(s16, w_ada, b_ada):
    depth = w_ada.shape[0]
    tn = 1536
    return pl.pallas_call(
        _mod_kernel,
        grid=(depth, 6 * D // tn),
        in_specs=[
            pl.BlockSpec((16, D), lambda l, n: (0, 0)),
            pl.BlockSpec((1, D, tn), lambda l, n: (l, 0, n)),
            pl.BlockSpec((1, 1, tn), lambda l, n: (l, 0, n)),
        ],
        out_specs=pl.BlockSpec((1, 16, tn), lambda l, n: (l, 0, n)),
        out_shape=jax.ShapeDtypeStruct((depth, 16, 6 * D), F32),
        compiler_params=_cparams(("parallel", "parallel")),
        name="mod",
    )(s16, w_ada, b_ada.reshape(depth, 1, 6 * D))


def _a1_kernel(x_ref, mod_ref, g_ref, win_ref, wdw_ref, bdw_ref, lng_ref, lnb_ref,
               a_ref, bp_ref, pad_ref, conv_ref, shf_ref, *, seg):
    tm = x_ref.shape[0]
    nseg = tm // seg
    h = _rms_mod(x_ref[...], g_ref[...], mod_ref[0, 0:1, :], mod_ref[0, 1:2, :])
    u = jnp.dot(h.astype(BF16), win_ref[...], preferred_element_type=F32)
    a = u[:, :D] * _sigmoid(u[:, D:2 * D])
    bp_ref[...] = u[:, 2 * D:]
    zeros16 = jnp.zeros((nseg, 16, D), F32)
    pad_ref[:, 0:16, :] = zeros16
    pad_ref[:, seg + 16:seg + 32, :] = zeros16
    pad_ref[:, 16:16 + seg, :] = a.reshape(nseg, seg, D)

    def seg_body(s, carry):
        for cs in range(D // LANES):
            sl = slice(cs * LANES, (cs + 1) * LANES)
            acc = jnp.zeros((seg, LANES), F32)
            for r in range(8):
                starts = [st for st in range(r, CONV_K + 1, 8) if st >= 1]
                lo = starts[0]
                nrow = starts[-1] - lo + seg
                if r:
                    shf_ref[r, 0:nrow, :] = pad_ref[s, pl.ds(lo, nrow), sl]
                    src = lambda st: shf_ref[r, st - lo:st - lo + seg, :]
                else:
                    src = lambda st: pad_ref[s, pl.ds(st, seg), sl]
                for st in starts:
                    acc = acc + wdw_ref[st - 1:st, sl] * src(st)
            conv_ref[pl.ds(pl.multiple_of(s * seg, seg), seg), sl] = acc + bdw_ref[:, sl]
        return carry

    lax.fori_loop(0, nseg, seg_body, 0)
    cv = conv_ref[...]
    mu = jnp.mean(cv, axis=-1, keepdims=True)
    cc = cv - mu
    var = jnp.mean(cc * cc, axis=-1, keepdims=True)
    y = cc * lax.rsqrt(var + EPS) * lng_ref[...] + lnb_ref[...]
    a_ref[...] = (y * _sigmoid(y)).astype(BF16)


def _a1_call(x2d, mod, g, win, wdw, bdw, lng, lnb, *, seq, tm, seg):
    n = x2d.shape[0]
    tps = seq // tm
    row = lambda v: v.reshape(1, -1)
    const = lambda shape: pl.BlockSpec(shape, lambda i: (0,) * len(shape))
    return pl.pallas_call(
        functools.partial(_a1_kernel, seg=seg),
        grid=(n // tm,),
        in_specs=[
            pl.BlockSpec((tm, D), lambda i: (i, 0)),
            pl.BlockSpec((1, 8, D), lambda i: ((i // tps) % mod.shape[0], 0, 0)),
            const((1, D)), const((D, 3 * D)), const((CONV_K, D)), const((1, D)), const((1, D)), const((1, D)),
        ],
        out_specs=[pl.BlockSpec((tm, D), lambda i: (i, 0)), pl.BlockSpec((tm, D), lambda i: (i, 0))],
        out_shape=[jax.ShapeDtypeStruct((n, D), BF16), jax.ShapeDtypeStruct((n, D), F32)],
        scratch_shapes=[pltpu.VMEM((tm // seg, seg + 32, D), F32), pltpu.VMEM((tm, D), F32),
                        pltpu.VMEM((8, seg + 24, LANES), F32)],
        compiler_params=_cparams(("parallel",)),
        name="a1",
    )(x2d, mod, row(g), win, wdw, row(bdw), row(lng), row(lnb))


def _store_tiles(ref, val):
    n = val.shape[0]
    for s in range(D // LANES):
        ref[pl.ds(s, n, stride=8), :] = val[:, s * LANES:(s + 1) * LANES]


def _load_tiles(ref, n):
    return jnp.concatenate([ref[pl.ds(s, n, stride=8), :] for s in range(D // LANES)], axis=-1)


def _tail(x, y, mod_ref, gffn_ref, wrT_ref, br_ref, x1_ref, h2_ref, cls_ref, rw_ref):
    tm = x.shape[0]
    x1 = x + mod_ref[0, 2:3, :] * y
    x1_ref[...] = x1
    h2 = _rms_mod(x1, gffn_ref[...], mod_ref[0, 3:4, :], mod_ref[0, 4:5, :])
    logits = lax.dot_general(wrT_ref[...], h2, (((1,), (1,)), ((), ())),
                             precision=HIGHEST, preferred_element_type=F32)
    sc = _sigmoid(logits)
    bi = sc + br_ref[...]
    gs = []
    for g in range(N_EXPERTS // EPG):
        v = [bi[EPG * g + j:EPG * g + j + 1, :] for j in range(EPG)]
        best = v[0] + v[1]
        for (p, q) in ((0, 2), (0, 3), (1, 2), (1, 3), (2, 3)):
            best = jnp.maximum(best, v[p] + v[q])
        gs.append(best)
    gsel = jnp.zeros((1, tm), I32)
    gbest = gs[0]
    for g in range(1, N_EXPERTS // EPG):
        better = gs[g] > gbest
        gsel = jnp.where(better, g, gsel)
        gbest = jnp.where(better, gs[g], gbest)

    def pick(arr, j):
        out = arr[j:j + 1, :]
        for g in range(1, N_EXPERTS // EPG):
            out = jnp.where(gsel == g, arr[EPG * g + j:EPG * g + j + 1, :], out)
        return out

    cand = [pick(bi, j) for j in range(EPG)]
    scs = [pick(sc, j) for j in range(EPG)]
    i1 = jnp.zeros((1, tm), I32)
    b1 = cand[0]
    for j in range(1, EPG):
        better = cand[j] > b1
        i1 = jnp.where(better, j, i1)
        b1 = jnp.where(better, cand[j], b1)
    neg = jnp.full((1, tm), -jnp.inf, F32)
    i2 = jnp.zeros((1, tm), I32)
    b2 = neg
    for j in range(EPG):
        cj = jnp.where(i1 == j, neg, cand[j])
        better = cj > b2
        i2 = jnp.where(better, j, i2)
        b2 = jnp.where(better, cj, b2)
    w1 = jnp.zeros((1, tm), F32)
    w2 = jnp.zeros((1, tm), F32)
    for j in range(EPG):
        w1 = jnp.where(i1 == j, scs[j], w1)
        w2 = jnp.where(i2 == j, scs[j], w2)
    den = w1 + w2
    w1 = w1 / den
    w2 = w2 / den
    first_lo = i1 < i2
    lo = jnp.minimum(i1, i2)
    hi = jnp.maximum(i1, i2)
    w_lo = jnp.where(first_lo, w1, w2)
    w_hi = jnp.where(first_lo, w2, w1)
    pidx = jnp.where(lo == 0, hi - 1, jnp.where(lo == 1, hi + 1, 5))
    cls = gsel * len(PAIR_LO) + pidx
    cls_ref[...] = jnp.broadcast_to(cls, (8, tm))
    rw_ref[...] = jnp.concatenate([w_lo, w_hi, jnp.zeros((6, tm), F32)], axis=0)
    _store_tiles(h2_ref, h2)


def _a2_kernel(bpp_ref, bpc_ref, bpn_ref, a_ref, x_ref, mod_ref, wpool_ref, bpool_ref, pscale_ref, wo_ref,
               gffn_ref, wrT_ref, br_ref, x1_ref, h2_ref, cls_ref, rw_ref, pb_ref, *, stride, npos, use_halo,
               ntile, nt):
    tm = x_ref.shape[0]
    halo = 8 * stride
    i = lax.rem(jnp.minimum(pl.program_id(0), ntile - 1), nt)
    cur = bpc_ref[...]
    if use_halo:
        pb_ref[0:halo, :] = jnp.where(i > 0, bpp_ref[...], 0.0)
        pb_ref[halo + tm:halo + tm + halo, :] = jnp.where(i < nt - 1, bpn_ref[...], 0.0)
    else:
        pb_ref[0:halo, :] = jnp.zeros((halo, D), F32)
        pb_ref[halo + tm:halo + tm + halo, :] = jnp.zeros((halo, D), F32)
    pb_ref[halo:halo + tm, :] = cur
    pos = (i * tm + lax.broadcasted_iota(I32, (tm, 1), 0)) // stride
    y = jnp.dot(a_ref[...], wo_ref[0:D, :], preferred_element_type=F32)
    for g, w in enumerate(POOL_WINDOWS):
        gsl = slice(g * POOL_GC, (g + 1) * POOL_GC)
        left = w // 2
        right = w - 1 - left
        tot = jnp.zeros((tm, POOL_GC), F32)
        for j in range(-left, right + 1):
            tot = tot + pb_ref[halo + j * stride:halo + j * stride + tm, gsl]
        cnt = (jnp.minimum(pos + right + 1, npos) - jnp.maximum(pos - left, 0)).astype(F32)
        pooled = tot / cnt - cur[:, gsl]
        pg = jnp.dot(pooled.astype(BF16), wpool_ref[g], preferred_element_type=F32) + bpool_ref[g:g + 1, :]
        pg = pg * pscale_ref[:, gsl]
        y = y + jnp.dot(pg.astype(BF16), wo_ref[D + g * POOL_GC:D + (g + 1) * POOL_GC, :],
                        preferred_element_type=F32)
    _tail(x_ref[...], y, mod_ref, gffn_ref, wrT_ref, br_ref, x1_ref, h2_ref, cls_ref, rw_ref)


def _a2_call(bp, a, x2d, mod, wpool, bpool, pscale, wo, gffn, wrT, br, h2_prev, *, seq, tm, stride, npos, use_halo,
             h2_rows, h2_row_off):
    n = x2d.shape[0]
    tps = seq // tm
    ntile = n // tm
    halo = 8 * stride
    off = h2_row_off // tm
    steps = ntile if h2_prev is not None else h2_rows // tm
    blk = lambda i: jnp.minimum(i, ntile - 1)
    row = lambda v: v.reshape(1, -1)
    const = lambda shape: pl.BlockSpec(shape, lambda i: (0,) * len(shape))
    in_specs = [
        pl.BlockSpec((tm, D), lambda i: (jnp.maximum(blk(i) - 1, 0), 0)),
        pl.BlockSpec((tm, D), lambda i: (blk(i), 0)),
        pl.BlockSpec((tm, D), lambda i: (jnp.minimum(blk(i) + 1, ntile - 1), 0)),
        pl.BlockSpec((tm, D), lambda i: (blk(i), 0)),
        pl.BlockSpec((tm, D), lambda i: (blk(i), 0)),
        pl.BlockSpec((1, 8, D), lambda i: ((blk(i) // tps) % mod.shape[0], 0, 0)),
        const((4, POOL_GC, POOL_GC)), const((4, POOL_GC)), const((1, D)), const((2 * D, D)),
        const((1, D)), const((N_EXPERTS, D)), const((N_EXPERTS, 1)),
    ]
    args = [bp, bp, bp, a, x2d, mod, wpool, bpool, row(pscale), wo, row(gffn), wrT, br.reshape(N_EXPERTS, 1)]
    aliases = {}
    body = functools.partial(_a2_kernel, stride=stride, npos=npos, use_halo=use_halo, ntile=ntile, nt=tps)
    if h2_prev is not None:
        in_specs.append(pl.BlockSpec(memory_space=pl.ANY))
        args.append(h2_prev)
        aliases = {len(args) - 1: 1}
        inner = body
        body = lambda *refs: inner(*refs[:13], *refs[14:])
    return pl.pallas_call(
        body,
        grid=(steps,),
        in_specs=in_specs,
        out_specs=[
            pl.BlockSpec((tm, D), lambda i: (blk(i), 0)),
            pl.BlockSpec((tm * 8, LANES), lambda i: (off + i, 0)),
            pl.BlockSpec((8, tm), lambda i: (0, blk(i))),
            pl.BlockSpec((8, tm), lambda i: (0, blk(i))),
        ],
        out_shape=[
            jax.ShapeDtypeStruct((n, D), F32),
            jax.ShapeDtypeStruct((h2_rows * 8, LANES), F32),
            jax.ShapeDtypeStruct((8, n), I32),
            jax.ShapeDtypeStruct((8, n), F32),
        ],
        scratch_shapes=[pltpu.VMEM((tm + 2 * halo, D), F32)],
        input_output_aliases=aliases,
        compiler_params=_cparams(("arbitrary",)),
        name="a2",
    )(*args)


def _moe_kernel(tlo_ref, thi_ref, tcnt_ref, idx_hbm, h2_hbm, rw_ref, w1lo, w3lo, w2lo, w1hi, w3hi, w2hi, y_hbm,
                idx_smem, xin, xb_ref, wt_ref, ybuf, sem_idx, sem_g, sem_s, *, tb, n_tiles):
    i = pl.program_id(0)
    last = n_tiles - 1
    unroll = 8

    def idx_copy(t):
        slot = lax.rem(t, 4)
        return pltpu.make_async_copy(idx_hbm.at[t], idx_smem.at[slot], sem_idx.at[slot])

    def tile_rows(j):
        return pl.ds(j * 8 if isinstance(j, int) else pl.multiple_of(j * 8, 8), 8)

    def gather_copy(row8, j):
        return pltpu.make_async_copy(h2_hbm.at[pl.ds(pl.multiple_of(row8, 8), 8), :], xin.at[tile_rows(j), :], sem_g)

    def gather_all():
        return pltpu.make_async_copy(h2_hbm.at[pl.ds(0, tb * 8), :], xin, sem_g)

    def scatter_copy(row8, slot2, j):
        return pltpu.make_async_copy(ybuf.at[slot2, tile_rows(j), :],
                                     y_hbm.at[pl.ds(pl.multiple_of(row8, 8), 8), :], sem_s.at[slot2])

    def rows(fn, count=tb):
        def body(jj, c):
            for u in range(unroll):
                fn(jj * unroll + u)
            return c
        full = count // unroll
        lax.fori_loop(0, full, body, 0)
        if not isinstance(count, int):
            def tail(j, c):
                fn(j)
                return c
            lax.fori_loop(full * unroll, count, tail, 0)

    def wait_scatter(t):
        s2 = lax.rem(t, 2)
        rows(lambda j: scatter_copy(0, s2, j).wait(), tcnt_ref[t])

    @pl.when(i == 0)
    def _():
        idx_copy(0).start()
        idx_copy(0).wait()
        rows(lambda j: gather_copy(idx_smem[0, 0, j], j).start())
        idx_copy(1).start()

    @pl.when(i < last)
    def _():
        idx_copy(i + 1).wait()

    @pl.when(i + 2 <= last)
    def _():
        idx_copy(i + 2).start()

    gather_all().wait()

    @pl.when(i >= 2)
    def _():
        wait_scatter(i - 2)

    for s in range(D // LANES):
        xb_ref[:, s * LANES:(s + 1) * LANES] = xin[pl.ds(s, tb, stride=8), :].astype(BF16)
    wt_ref[...] = jnp.zeros(wt_ref.shape, F32)
    wt_ref[0:8, :] = rw_ref[0]
    wcol = wt_ref[...].T
    s4n = lax.rem(jnp.minimum(i + 1, last), 4)
    s2 = lax.rem(i, 2)
    tp = jnp.maximum(i - 1, 0)
    cntp = jnp.where(i >= 1, tcnt_ref[tp], 0)
    s4p = lax.rem(tp, 4)
    for j in range(tb):
        gather_copy(idx_smem[s4n, 0, j], j).start()
    for j in range(tb):
        @pl.when(j < cntp)
        def _():
            scatter_copy(idx_smem[s4p, 0, j], 1 - s2, j).start()
    xb = xb_ref[...]

    def ffn(w1, w3, w2):
        h1 = jnp.dot(xb, w1[0], preferred_element_type=F32)
        h3 = jnp.dot(xb, w3[0], preferred_element_type=F32)
        act = (h1 * _sigmoid(h1)) * h3
        return jnp.dot(act.astype(BF16), w2[0], preferred_element_type=F32)

    y = wcol[:, 0:1] * ffn(w1lo, w3lo, w2lo) + wcol[:, 1:2] * ffn(w1hi, w3hi, w2hi)
    _store_tiles(ybuf.at[s2], y)

    @pl.when(i == last)
    def _():
        wait_scatter(i - 1)
        gather_all().wait()


def _moe_plan(cls, n_tok, tb):
    n_tiles = -(-n_tok // tb) + N_CLASSES
    classes = jnp.arange(N_CLASSES, dtype=I32)
    sorted_cls, order = lax.sort_key_val(cls, jnp.arange(n_tok, dtype=I32), is_stable=True)
    cstart = jnp.searchsorted(sorted_cls, classes, side="left").astype(I32)
    counts = jnp.searchsorted(sorted_cls, classes, side="right").astype(I32) - cstart
    tpc = (counts + tb - 1) // tb
    tend = jnp.cumsum(tpc)
    tstart = tend - tpc
    tile_ids = jnp.arange(n_tiles, dtype=I32)
    tile_cls = jnp.minimum(jnp.searchsorted(tend, tile_ids, side="right"), N_CLASSES - 1).astype(I32)
    first = (tile_ids - tstart[tile_cls]) * tb
    tcnt = jnp.clip(counts[tile_cls] - first, 0, tb)
    off = jnp.arange(tb, dtype=I32)[None, :]
    pos = (cstart[tile_cls] + first)[:, None] + off
    valid = off < tcnt[:, None]
    src = jnp.where(valid, order[jnp.clip(pos, 0, n_tok - 1)], 0)
    grp = tile_cls // len(PAIR_LO)
    pair = tile_cls % len(PAIR_LO)
    tlo = grp * EPG + jnp.asarray(PAIR_LO, I32)[pair]
    thi = grp * EPG + jnp.asarray(PAIR_HI, I32)[pair]
    return tlo.astype(I32), thi.astype(I32), tcnt.astype(I32), src, n_tiles


def _moe_call(h2t, cls, rw, w1, w3, w2, *, tb=MOE_TB):
    n_tok = cls.shape[0]
    assert n_tok % tb == 0
    tlo, thi, tcnt, src, n_tiles = _moe_plan(cls, n_tok, tb)
    idx = (src * 8).reshape(n_tiles, 1, tb)
    rw_sorted = jnp.take(rw[0:2], src.reshape(-1), axis=1).reshape(2, n_tiles, tb).transpose(1, 0, 2)
    rw_sorted = jnp.pad(rw_sorted, ((0, 0), (0, 6), (0, 0)))
    wspec_lo = pl.BlockSpec((1, D, D), lambda i, lo, hi, cnt: (lo[i], 0, 0))
    wspec_hi = pl.BlockSpec((1, D, D), lambda i, lo, hi, cnt: (hi[i], 0, 0))
    grid_spec = pltpu.PrefetchScalarGridSpec(
        num_scalar_prefetch=3,
        grid=(n_tiles,),
        in_specs=[pl.BlockSpec(memory_space=pl.ANY), pl.BlockSpec(memory_space=pl.ANY),
                  pl.BlockSpec((1, 8, tb), lambda i, lo, hi, cnt: (i, 0, 0)),
                  wspec_lo, wspec_lo, wspec_lo, wspec_hi, wspec_hi, wspec_hi],
        out_specs=pl.BlockSpec(memory_space=pl.ANY),
        scratch_shapes=[
            pltpu.SMEM((4, 1, tb), I32),
            pltpu.VMEM((tb * 8, LANES), F32),
            pltpu.VMEM((tb, D), BF16),
            pltpu.VMEM((LANES, tb), F32),
            pltpu.VMEM((2, tb * 8, LANES), F32),
            pltpu.SemaphoreType.DMA((4,)),
            pltpu.SemaphoreType.DMA(()),
            pltpu.SemaphoreType.DMA((2,)),
        ],
    )
    return pl.pallas_call(
        functools.partial(_moe_kernel, tb=tb, n_tiles=n_tiles),
        grid_spec=grid_spec,
        out_shape=jax.ShapeDtypeStruct((n_tok * 8, LANES), F32),
        compiler_params=_cparams(("arbitrary",)),
        name="moe",
    )(tlo, thi, tcnt, idx, h2t, rw_sorted, w1, w3, w2, w1, w3, w2)


def _m1_kernel(xp_ref, xc_ref, xn_ref, yp_ref, yc_ref, yn_ref, modp_ref, modc_ref, g_ref, win_ref, wconv_ref,
               bconv_ref, bdq_ref, bdk_ref, bdv_ref, wgT_ref, bg_ref,
               x2_ref, q_ref, k_ref, v_ref, o_ref, gT_ref, xm_ref):
    tm = xc_ref.shape[0]
    i = pl.program_id(1)
    nt = pl.num_programs(1)
    xe = jnp.concatenate([xp_ref[...], xc_ref[...], xn_ref[...]], axis=0)
    ye = jnp.concatenate([_load_tiles(yp_ref, 8), _load_tiles(yc_ref, tm), _load_tiles(yn_ref, 8)], axis=0)
    xe = xe + modp_ref[0, 5:6, :] * ye
    x2_ref[...] = xe[8:8 + tm]
    h = _rms_mod(xe, g_ref[...], modc_ref[0, 0:1, :], modc_ref[0, 1:2, :]).astype(BF16)
    xm = jnp.dot(h, win_ref[:, 0:E], preferred_element_type=F32)
    z = jnp.dot(h, win_ref[:, E:2 * E], preferred_element_type=F32)
    o_ref[...] = _sigmoid(z[8:8 + tm]).astype(BF16)
    rid = lax.broadcasted_iota(I32, (tm + 16, 1), 0)
    inside = jnp.logical_and(jnp.logical_or(rid >= 8, i > 0), jnp.logical_or(rid < tm + 8, i < nt - 1))
    xm = jnp.where(inside, xm, 0.0)
    xm_ref[...] = xm
    acc = jnp.zeros((tm, E), F32) + bconv_ref[...]
    for kk in range(SHORT_K):
        acc = acc + wconv_ref[kk:kk + 1, :] * xm_ref[6 + kk:6 + kk + tm, :]
    xcb = (acc * _sigmoid(acc)).astype(BF16)
    xmb = xm[8:8 + tm].astype(BF16)
    nt_dims = (((1,), (1,)), ((), ()))
    gT = jnp.zeros((4 * NH, tm), F32) + bg_ref[...]
    kscale = DH ** -0.5
    for j in range(E // MXU_DIM):
        sl = slice(j * MXU_DIM, (j + 1) * MXU_DIM)
        qj = jnp.dot(xcb[:, sl], bdq_ref[j], preferred_element_type=F32)
        kj = jnp.dot(xcb[:, sl], bdk_ref[j], preferred_element_type=F32)
        vj = jnp.dot(xmb[:, sl], bdv_ref[j], preferred_element_type=F32)
        qb, kb, vb = qj.astype(BF16), kj.astype(BF16), vj.astype(BF16)
        gT = gT + lax.dot_general(wgT_ref[:, sl], qb, nt_dims, preferred_element_type=F32)
        gT = gT + lax.dot_general(wgT_ref[:, E + j * MXU_DIM:E + (j + 1) * MXU_DIM], kb, nt_dims,
                                  preferred_element_type=F32)
        gT = gT + lax.dot_general(wgT_ref[:, 2 * E + j * MXU_DIM:2 * E + (j + 1) * MXU_DIM], vb, nt_dims,
                                  preferred_element_type=F32)
        q_ref[:, sl] = qb
        k_ref[:, sl] = (kj * kscale).astype(BF16)
        v_ref[:, sl] = vb
    gT_ref[...] = gT


def _m1_call(xa, ya, modp, modc, g, win, wconv, bconv, bdq, bdk, bdv, wgT, bg, *, nb, seq, tm, y_row_off):
    n = nb * seq
    tps = seq // tm
    hb = tm // 8
    nb8 = n // 8
    yo = y_row_off // tm
    yo8 = y_row_off // 8
    blk = lambda b, i: b * tps + i
    const = lambda shape: pl.BlockSpec(shape, lambda b, i: (0,) * len(shape))
    prev8 = lambda b, i: jnp.maximum(blk(b, i) * hb - 1, 0)
    next8 = lambda b, i: jnp.minimum((blk(b, i) + 1) * hb, nb8 - 1)
    in_specs = [
        pl.BlockSpec((8, D), lambda b, i: (prev8(b, i), 0)),
        pl.BlockSpec((tm, D), lambda b, i: (blk(b, i), 0)),
        pl.BlockSpec((8, D), lambda b, i: (next8(b, i), 0)),
        pl.BlockSpec((8 * 8, LANES), lambda b, i: (yo8 + prev8(b, i), 0)),
        pl.BlockSpec((tm * 8, LANES), lambda b, i: (yo + blk(b, i), 0)),
        pl.BlockSpec((8 * 8, LANES), lambda b, i: (yo8 + next8(b, i), 0)),
        pl.BlockSpec((1, 8, D), lambda b, i: (b % modp.shape[0], 0, 0)),
        pl.BlockSpec((1, 8, D), lambda b, i: (b % modc.shape[0], 0, 0)),
        const((1, D)), const((D, 2 * E)), const((SHORT_K, E)), const((1, E)),
        const((E // MXU_DIM, MXU_DIM, MXU_DIM)), const((E // MXU_DIM, MXU_DIM, MXU_DIM)),
        const((E // MXU_DIM, MXU_DIM, MXU_DIM)), const((4 * NH, 3 * E)), const((4 * NH, 1)),
    ]
    tok = lambda w, dt: jax.ShapeDtypeStruct((n, w), dt)
    tspec = lambda w: pl.BlockSpec((tm, w), lambda b, i: (blk(b, i), 0))
    return pl.pallas_call(
        _m1_kernel,
        grid=(nb, tps),
        in_specs=in_specs,
        out_specs=[tspec(D), tspec(E), tspec(E), tspec(E), tspec(E),
                   pl.BlockSpec((4 * NH, tm), lambda b, i: (0, blk(b, i)))],
        out_shape=[tok(D, F32), tok(E, BF16), tok(E, BF16), tok(E, BF16), tok(E, BF16),
                   jax.ShapeDtypeStruct((4 * NH, n), F32)],
        scratch_shapes=[pltpu.VMEM((tm + 16, E), F32)],
        compiler_params=_cparams(("parallel", "arbitrary")),
        name="m1",
    )(xa, xa, xa, ya, ya, ya, modp, modc, g.reshape(1, D), win, wconv, bconv.reshape(1, E), bdq, bdk, bdv, wgT,
      bg.reshape(4 * NH, 1))


def _log_sigmoid(x):
    return jnp.minimum(x, 0.0) - jnp.log1p(jnp.exp(-jnp.abs(x)))


def _scan_masks(t, rev):
    r = lax.broadcasted_iota(I32, (t, t), 0)
    c = lax.broadcasted_iota(I32, (t, t), 1)
    tri = (c >= r) if rev else (c <= r)
    tri_t = (r >= c) if rev else (r <= c)
    return tri, tri_t, c == r


def _scan_step(q, k, v, i_row, f_row, c_ref, n_ref, m_ref, masks):
    t = q.shape[0]
    tri, tri_t, eye = masks
    lf = _log_sigmoid(f_row)
    lf_b = jnp.broadcast_to(lf, (t, t))
    b_col = jnp.sum(jnp.where(tri, lf_b, 0.0), axis=1, keepdims=True)
    lf_col = jnp.sum(jnp.where(eye, lf_b, 0.0), axis=1, keepdims=True)
    i_col = jnp.sum(jnp.where(eye, jnp.broadcast_to(i_row, (t, t)), 0.0), axis=1, keepdims=True)
    b_row = jnp.sum(jnp.where(tri_t, jnp.broadcast_to(lf_col, (t, t)), 0.0), axis=0, keepdims=True)
    m = m_ref[0:1, 0:1]
    dmat = jnp.where(tri, b_col - b_row + i_row, -jnp.inf)
    inter = b_col + m
    m_t = jnp.maximum(inter, jnp.max(dmat, axis=1, keepdims=True))
    wmat = jnp.exp(dmat - m_t)
    w_inter = jnp.exp(inter - m_t)
    s = lax.dot_general(q, k, (((1,), (1,)), ((), ())), preferred_element_type=F32) * wmat
    cb = c_ref[...].astype(BF16)
    num = jnp.dot(s.astype(BF16), v, preferred_element_type=F32) \
        + w_inter * jnp.dot(q, cb, preferred_element_type=F32)
    qn = jnp.sum(q.astype(F32) * n_ref[...], axis=1, keepdims=True)
    den = jnp.sum(s, axis=1, keepdims=True) + w_inter * qn
    h = num / jnp.maximum(jnp.abs(den), jnp.exp(-m_t))
    b_end = jnp.sum(lf, axis=1, keepdims=True)
    dlog = b_end - b_col + i_col
    m_new = jnp.maximum(b_end + m, jnp.max(dlog, axis=0, keepdims=True))
    wk = jnp.exp(dlog - m_new)
    decay = jnp.exp(b_end + m - m_new)
    kf = k.astype(F32) * wk
    c_ref[...] = decay * c_ref[...] + lax.dot_general(kf.astype(BF16), v, (((0,), (0,)), ((), ())),
                                                      preferred_element_type=F32)
    n_ref[...] = decay * n_ref[...] + jnp.sum(kf, axis=0, keepdims=True)
    m_ref[...] = jnp.broadcast_to(m_new, m_ref.shape)
    return h


def _scan_kernel(*refs, rev, final):
    qc, kc, vc, ic, fc, ql, kl, vl, il, fl = refs[:10]
    if final:
        hf_ref, o_ref, hng_ref, out_ref, c_ref, n_ref, m_ref = refs[10:]
    else:
        out_ref, c_ref, n_ref, m_ref = refs[10:]
    j = pl.program_id(2)
    masks = _scan_masks(CHUNK_T, rev)
    heads = [slice(hh * DH, (hh + 1) * DH) for hh in range(SCAN_HEADS)]

    def step(qr, kr, vr, ir, fr, hh):
        hs = heads[hh]
        return _scan_step(qr[0, :, hs], kr[0, :, hs], vr[0, :, hs], ir[hh], fr[hh],
                          c_ref.at[hh], n_ref.at[hh], m_ref.at[hh], masks)

    @pl.when(j == 0)
    def _():
        c_ref[...] = jnp.zeros(c_ref.shape, F32)
        n_ref[...] = jnp.zeros(n_ref.shape, F32)
        m_ref[...] = jnp.zeros(m_ref.shape, F32)
        for hh in range(SCAN_HEADS):
            step(qc, kc, vc, ic, fc, hh)

    @pl.when(j > 0)
    def _():
        for hh in range(SCAN_HEADS):
            h = step(ql, kl, vl, il, fl, hh)
            if final:
                hs = h + hf_ref[0, :, heads[hh]]
                mu = jnp.mean(hs, axis=-1, keepdims=True)
                hc = hs - mu
                var = jnp.mean(hc * hc, axis=-1, keepdims=True)
                hn = hc * lax.rsqrt(var + EPS) * hng_ref[:, heads[hh]]
                out_ref[0, :, heads[hh]] = (o_ref[0, :, heads[hh]] * hn.astype(BF16)).astype(BF16)
            else:
                out_ref[0, :, heads[hh]] = h


def _scan_call(qc, kc, vc, gc, ql, kl, vl, gl, hf, o, hng, *, nb, seq, rev):
    final = rev
    t = CHUNK_T
    ncl = seq // t
    if rev:
        cidx = lambda j: ncl - jnp.maximum(j, 1)
        irow, frow = 2 * NH, 3 * NH
    else:
        cidx = lambda j: jnp.maximum(j - 1, 0)
        irow, frow = 0, NH
    hp = SCAN_HEADS
    wid = hp * DH
    cspec = pl.BlockSpec((1, t, wid), lambda b, h, j: (b, 0, h))
    lspec = pl.BlockSpec((1, t, wid), lambda b, h, j: (b, cidx(j), h))
    gcs = lambda r0: pl.BlockSpec((hp, 1, t), lambda b, h, j: (r0 // hp + h, 0, b))
    gls = lambda r0: pl.BlockSpec((hp, 1, t), lambda b, h, j: (r0 // hp + h, 0, b * ncl + cidx(j)))
    in_specs = [cspec, cspec, cspec, gcs(irow), gcs(frow), lspec, lspec, lspec, gls(irow), gls(frow)]
    args = [qc, kc, vc, gc, gc, ql, kl, vl, gl, gl]
    if final:
        in_specs += [lspec, lspec, pl.BlockSpec((1, wid), lambda b, h, j: (0, h))]
        args += [hf, o, hng.reshape(1, E)]
    return pl.pallas_call(
        functools.partial(_scan_kernel, rev=rev, final=final),
        grid=(nb, NH // hp, ncl + 1),
        in_specs=in_specs,
        out_specs=lspec,
        out_shape=jax.ShapeDtypeStruct((nb, seq, E), BF16 if final else F32),
        scratch_shapes=[pltpu.VMEM((hp, DH, DH), F32), pltpu.VMEM((hp, 1, DH), F32),
                        pltpu.VMEM((hp, 8, LANES), F32)],
        compiler_params=_cparams(("parallel", "parallel", "arbitrary")),
        name="scan_bwd" if rev else "scan_fwd",
    )(*args)


def _m3_kernel(hg_ref, x_ref, mod_ref, wo_ref, gffn_ref, wrT_ref, br_ref, x1_ref, h2_ref, cls_ref, rw_ref):
    y = jnp.dot(hg_ref[...], wo_ref[...], preferred_element_type=F32)
    _tail(x_ref[...], y, mod_ref, gffn_ref, wrT_ref, br_ref, x1_ref, h2_ref, cls_ref, rw_ref)


def _m3_call(hg, x2d, mod, wo, gffn, wrT, br, *, seq, tm):
    n = x2d.shape[0]
    tps = seq // tm
    const = lambda shape: pl.BlockSpec(shape, lambda i: (0,) * len(shape))
    return pl.pallas_call(
        _m3_kernel,
        grid=(n // tm,),
        in_specs=[
            pl.BlockSpec((tm, E), lambda i: (i, 0)),
            pl.BlockSpec((tm, D), lambda i: (i, 0)),
            pl.BlockSpec((1, 8, D), lambda i: (i // tps, 0, 0)),
            const((E, D)), const((1, D)), const((N_EXPERTS, D)), const((N_EXPERTS, 1)),
        ],
        out_specs=[
            pl.BlockSpec((tm, D), lambda i: (i, 0)),
            pl.BlockSpec((tm * 8, LANES), lambda i: (i, 0)),
            pl.BlockSpec((8, tm), lambda i: (0, i)),
            pl.BlockSpec((8, tm), lambda i: (0, i)),
        ],
        out_shape=[
            jax.ShapeDtypeStruct((n, D), F32),
            jax.ShapeDtypeStruct((n * 8, LANES), F32),
            jax.ShapeDtypeStruct((8, n), I32),
            jax.ShapeDtypeStruct((8, n), F32),
        ],
        compiler_params=_cparams(("parallel",)),
        name="m3",
    )(hg, x2d, mod, wo, gffn.reshape(1, D), wrT, br.reshape(N_EXPERTS, 1))


def _final_kernel(x_ref, y_ref, mod_ref, g_ref, o_ref):
    x = x_ref[...] + mod_ref[0, 5:6, :] * _load_tiles(y_ref, x_ref.shape[0])
    ms = jnp.mean(x * x, axis=-1, keepdims=True)
    o_ref[...] = x * lax.rsqrt(ms + EPS) * g_ref[...]


def _final_call(x2d, y, mod, g, *, seq, tm):
    n = x2d.shape[0]
    tps = seq // tm
    return pl.pallas_call(
        _final_kernel,
        grid=(n // tm,),
        in_specs=[
            pl.BlockSpec((tm, D), lambda i: (i, 0)),
            pl.BlockSpec((tm * 8, LANES), lambda i: (i, 0)),
            pl.BlockSpec((1, 8, D), lambda i: (i // tps, 0, 0)),
            pl.BlockSpec((1, D), lambda i: (0, 0)),
        ],
        out_specs=pl.BlockSpec((tm, D), lambda i: (i, 0)),
        out_shape=jax.ShapeDtypeStruct((n, D), F32),
        compiler_params=_cparams(("parallel",)),
        name="final",
    )(x2d, y, mod, g.reshape(1, D))


def _blockdiag_tiles(w):
    per = MXU_DIM // 4
    w = w.reshape(E // MXU_DIM, per, 4, 4)
    eye = jnp.eye(per, dtype=w.dtype)
    t = w[:, :, :, None, :] * eye[None, :, None, :, None]
    return t.reshape(E // MXU_DIM, MXU_DIM, MXU_DIM)


def kernel(x, c, ctx, c_ctx, w_ada, b_ada, g_mix, g_ffn, g_final, w_in_ab, w_dw_a, b_dw_a, ln_g_a, ln_b_a, w_pool,
           b_pool, pool_scale, w_out_ab, w_in_c, w_conv_c, b_conv_c, w_q, w_k, w_v, w_gates, b_gates, hn_g, w_out_c,
           w_router, b_router, w1, w3, w2):
    nb, seq, _ = x.shape
    ctx_len = ctx.shape[1]
    assert ctx_len == CHUNK_T and seq % TM == 0 and seq % GRID_W == 0 and nb < 16
    n_lat = nb * seq
    n_ctx = nb * ctx_len
    x_lat = x.reshape(n_lat, D)
    x_ctx = ctx.reshape(n_ctx, D)

    s16 = jnp.zeros((16, D), F32).at[:nb].set(c).at[nb].set(c_ctx)
    mod = _mod_call(s16, w_ada, b_ada)
    pad8 = lambda m: jnp.pad(m.reshape(-1, 6, D), ((0, 0), (0, 2), (0, 0)))
    mod_lat = [pad8(mod[l, :nb]) for l in range(2)]
    mod_ctx = [pad8(mod[l, nb:nb + 1]) for l in range(2)]

    wrT = w_router.T
    bf = lambda w: w.astype(BF16)

    prm1 = (g_mix[0], bf(w_in_ab[0]), w_dw_a[0], b_dw_a[0], ln_g_a[0], ln_b_a[0])
    a_lat, bp_lat = _a1_call(x_lat, mod_lat[0], *prm1, seq=seq, tm=TM, seg=GRID_W)
    a_ctx, bp_ctx = _a1_call(x_ctx, mod_ctx[0], *prm1, seq=ctx_len, tm=ctx_len, seg=ctx_len)
    prm2 = (bf(w_pool[0]), b_pool[0], pool_scale[0], bf(w_out_ab[0]), g_ffn[0], wrT, b_router)
    n_tot = n_lat + n_ctx
    x1_lat, h2t, cls_lat, rw_lat = _a2_call(bp_lat, a_lat, x_lat, mod_lat[0], *prm2, None, seq=seq, tm=TM,
                                            stride=GRID_W, npos=seq // GRID_W, use_halo=True, h2_rows=n_tot,
                                            h2_row_off=0)
    x1_ctx, h2t, cls_ctx, rw_ctx = _a2_call(bp_ctx, a_ctx, x_ctx, mod_ctx[0], *prm2, h2t, seq=ctx_len, tm=ctx_len,
                                            stride=1, npos=ctx_len, use_halo=False, h2_rows=n_tot,
                                            h2_row_off=n_lat)
    cls0 = jnp.concatenate([cls_lat[0], cls_ctx[0]])
    rw0 = jnp.concatenate([rw_lat, rw_ctx], axis=1)
    y0 = _moe_call(h2t, cls0, rw0, bf(w1[0]), bf(w3[0]), bf(w2[0]))

    wgT = bf(w_gates[0].T)
    prm3 = (g_mix[1], bf(w_in_c[0]), w_conv_c[0], b_conv_c[0], bf(_blockdiag_tiles(w_q[0])),
            bf(_blockdiag_tiles(w_k[0])), bf(_blockdiag_tiles(w_v[0])), wgT, b_gates[0])
    x2_lat, ql, kl, vl, ol, gl = _m1_call(x1_lat, y0, mod_lat[0], mod_lat[1], *prm3, nb=nb, seq=seq, tm=TM1,
                                          y_row_off=0)
    _, qc, kc, vc, _, gc = _m1_call(x1_ctx, y0, mod_ctx[0], mod_ctx[1], *prm3, nb=nb, seq=ctx_len, tm=ctx_len,
                                    y_row_off=n_lat)
    r3 = lambda t, length: t.reshape(nb, length, E)
    ql, kl, vl, ol = (r3(t, seq) for t in (ql, kl, vl, ol))
    qc, kc, vc = (r3(t, ctx_len) for t in (qc, kc, vc))
    gl = gl.reshape(4 * NH, 1, n_lat)
    gc = gc.reshape(4 * NH, 1, n_ctx)
    hf = _scan_call(qc, kc, vc, gc, ql, kl, vl, gl, None, None, None, nb=nb, seq=seq, rev=False)
    hg = _scan_call(qc, kc, vc, gc, ql, kl, vl, gl, hf, ol, hn_g[0], nb=nb, seq=seq, rev=True)
    x3_lat, h2t1, cls1, rw1 = _m3_call(hg.reshape(n_lat, E), x2_lat, mod_lat[1], bf(w_out_c[0]), g_ffn[1], wrT,
                                       b_router, seq=seq, tm=TM)
    y1 = _moe_call(h2t1, cls1[0], rw1, bf(w1[1]), bf(w3[1]), bf(w2[1]))
    out = _final_call(x3_lat, y1, mod_lat[1], g_final, seq=seq, tm=TM)
    return out.reshape(nb, seq, D)
```

```python
import functools

import jax
import jax.numpy as jnp
from jax import lax
from jax.experimental import pallas as pl
from jax.experimental.pallas import tpu as pltpu

F32 = jnp.float32
BF16 = jnp.bfloat16
I32 = jnp.int32
HIGHEST = lax.Precision.HIGHEST

D = 1024
EPS = 1e-6
GRID_W = 64
CONV_K = 31
POOL_WINDOWS = (2, 4, 8, 16)
POOL_GC = 256
E = 2 * D
NH = 4
DH = E // NH
SHORT_K = 4
N_EXPERTS = 16
EPG = 4
N_CLASSES = 24
PAIR_LO = (0, 0, 0, 1, 1, 2)
PAIR_HI = (1, 2, 3, 2, 3, 3)

LANES = 128
MXU_DIM = 256
TM = 512
TM1 = 256
CHUNK_T = 256
SCAN_HEADS = 4
MOE_TB = 256
IDX_RING = 8
VMEM_LIMIT = 56 * 1024 * 1024


def _cparams(sem):
    return pltpu.CompilerParams(dimension_semantics=sem, vmem_limit_bytes=VMEM_LIMIT)


def _sigmoid(x):
    return jax.nn.sigmoid(x)


def _rms_mod(x, g, shift, scale):
    ms = jnp.mean(x * x, axis=-1, keepdims=True)
    return (x * lax.rsqrt(ms + EPS) * g) * (1.0 + scale) + shift


def _mod_kernel(s_ref, w_ref, b_ref, o_ref):
    s = s_ref[...]
    s = s * _sigmoid(s)
    o_ref[0] = jnp.dot(s, w_ref[0], preferred_element_type=F32, precision=HIGHEST) + b_ref[0]


def _mod_call(s16, w_ada, b_ada):
    depth = w_ada.shape[0]
    tn = 1536
    return pl.pallas_call(
        _mod_kernel,
        grid=(depth, 6 * D // tn),
        in_specs=[
            pl.BlockSpec((16, D), lambda l, n: (0, 0)),
            pl.BlockSpec((1, D, tn), lambda l, n: (l, 0, n)),
            pl.BlockSpec((1, 1, tn), lambda l, n: (l, 0, n)),
        ],
        out_specs=pl.BlockSpec((1, 16, tn), lambda l, n: (l, 0, n)),
        out_shape=jax.ShapeDtypeStruct((depth, 16, 6 * D), F32),
        compiler_params=_cparams(("parallel", "parallel")),
        name="mod",
    )(s16, w_ada, b_ada.reshape(depth, 1, 6 * D))


def _a1_kernel(x_ref, mod_ref, g_ref, win_ref, wdw_ref, bdw_ref, lng_ref, lnb_ref,
               a_ref, bp_ref, pad_ref, conv_ref, shf_ref, *, seg):
    tm = x_ref.shape[0]
    nseg = tm // seg
    h = _rms_mod(x_ref[...], g_ref[...], mod_ref[0, 0:1, :], mod_ref[0, 1:2, :])
    u = jnp.dot(h.astype(BF16), win_ref[...], preferred_element_type=F32)
    a = u[:, :D] * _sigmoid(u[:, D:2 * D])
    bp_ref[...] = u[:, 2 * D:]
    zeros16 = jnp.zeros((nseg, 16, D), F32)
    pad_ref[:, 0:16, :] = zeros16
    pad_ref[:, seg + 16:seg + 32, :] = zeros16
    pad_ref[:, 16:16 + seg, :] = a.reshape(nseg, seg, D)

    def seg_body(s, carry):
        for cs in range(D // LANES):
            sl = slice(cs * LANES, (cs + 1) * LANES)
            acc = jnp.zeros((seg, LANES), F32)
            for r in range(8):
                starts = [st for st in range(r, CONV_K + 1, 8) if st >= 1]
                lo = starts[0]
                nrow = starts[-1] - lo + seg
                if r:
                    shf_ref[r, 0:nrow, :] = pad_ref[s, pl.ds(lo, nrow), sl]
                    src = lambda st: shf_ref[r, st - lo:st - lo + seg, :]
                else:
                    src = lambda st: pad_ref[s, pl.ds(st, seg), sl]
                for st in starts:
                    acc = acc + wdw_ref[st - 1:st, sl] * src(st)
            conv_ref[pl.ds(pl.multiple_of(s * seg, seg), seg), sl] = acc + bdw_ref[:, sl]
        return carry

    lax.fori_loop(0, nseg, seg_body, 0)
    cv = conv_ref[...]
    mu = jnp.mean(cv, axis=-1, keepdims=True)
    cc = cv - mu
    var = jnp.mean(cc * cc, axis=-1, keepdims=True)
    y = cc * lax.rsqrt(var + EPS) * lng_ref[...] + lnb_ref[...]
    a_ref[...] = (y * _sigmoid(y)).astype(BF16)


def _a1_call(x2d, mod, g, win, wdw, bdw, lng, lnb, *, seq, tm, seg):
    n = x2d.shape[0]
    tps = seq // tm
    row = lambda v: v.reshape(1, -1)
    const = lambda shape: pl.BlockSpec(shape, lambda i: (0,) * len(shape))
    return pl.pallas_call(
        functools.partial(_a1_kernel, seg=seg),
        grid=(n // tm,),
        in_specs=[
            pl.BlockSpec((tm, D), lambda i: (i, 0)),
            pl.BlockSpec((1, 8, D), lambda i: ((i // tps) % mod.shape[0], 0, 0)),
            const((1, D)), const((D, 3 * D)), const((CONV_K, D)), const((1, D)), const((1, D)), const((1, D)),
        ],
        out_specs=[pl.BlockSpec((tm, D), lambda i: (i, 0)), pl.BlockSpec((tm, D), lambda i: (i, 0))],
        out_shape=[jax.ShapeDtypeStruct((n, D), BF16), jax.ShapeDtypeStruct((n, D), F32)],
        scratch_shapes=[pltpu.VMEM((tm // seg, seg + 32, D), F32), pltpu.VMEM((tm, D), F32),
                        pltpu.VMEM((8, seg + 24, LANES), F32)],
        compiler_params=_cparams(("parallel",)),
        name="a1",
    )(x2d, mod, row(g), win, wdw, row(bdw), row(lng), row(lnb))


def _store_tiles(ref, val):
    n = val.shape[0]
    for s in range(D // LANES):
        ref[pl.ds(s, n, stride=8), :] = val[:, s * LANES:(s + 1) * LANES]


def _load_tiles(ref, n):
    return jnp.concatenate([ref[pl.ds(s, n, stride=8), :] for s in range(D // LANES)], axis=-1)


def _tail(x, y, mod_ref, gffn_ref, wrT_ref, br_ref, x1_ref, h2_ref, cls_ref, rw_ref):
    tm = x.shape[0]
    x1 = x + mod_ref[0, 2:3, :] * y
    x1_ref[...] = x1
    h2 = _rms_mod(x1, gffn_ref[...], mod_ref[0, 3:4, :], mod_ref[0, 4:5, :])
    logits = lax.dot_general(wrT_ref[...], h2, (((1,), (1,)), ((), ())),
                             precision=HIGHEST, preferred_element_type=F32)
    sc = _sigmoid(logits)
    bi = sc + br_ref[...]
    gs = []
    for g in range(N_EXPERTS // EPG):
        v = [bi[EPG * g + j:EPG * g + j + 1, :] for j in range(EPG)]
        best = v[0] + v[1]
        for (p, q) in ((0, 2), (0, 3), (1, 2), (1, 3), (2, 3)):
            best = jnp.maximum(best, v[p] + v[q])
        gs.append(best)
    gsel = jnp.zeros((1, tm), I32)
    gbest = gs[0]
    for g in range(1, N_EXPERTS // EPG):
        better = gs[g] > gbest
        gsel = jnp.where(better, g, gsel)
        gbest = jnp.where(better, gs[g], gbest)

    def pick(arr, j):
        out = arr[j:j + 1, :]
        for g in range(1, N_EXPERTS // EPG):
            out = jnp.where(gsel == g, arr[EPG * g + j:EPG * g + j + 1, :], out)
        return out

    cand = [pick(bi, j) for j in range(EPG)]
    scs = [pick(sc, j) for j in range(EPG)]
    i1 = jnp.zeros((1, tm), I32)
    b1 = cand[0]
    for j in range(1, EPG):
        better = cand[j] > b1
        i1 = jnp.where(better, j, i1)
        b1 = jnp.where(better, cand[j], b1)
    neg = jnp.full((1, tm), -jnp.inf, F32)
    i2 = jnp.zeros((1, tm), I32)
    b2 = neg
    for j in range(EPG):
        cj = jnp.where(i1 == j, neg, cand[j])
        better = cj > b2
        i2 = jnp.where(better, j, i2)
        b2 = jnp.where(better, cj, b2)
    w1 = jnp.zeros((1, tm), F32)
    w2 = jnp.zeros((1, tm), F32)
    for j in range(EPG):
        w1 = jnp.where(i1 == j, scs[j], w1)
        w2 = jnp.where(i2 == j, scs[j], w2)
    den = w1 + w2
    w1 = w1 / den
    w2 = w2 / den
    first_lo = i1 < i2
    lo = jnp.minimum(i1, i2)
    hi = jnp.maximum(i1, i2)
    w_lo = jnp.where(first_lo, w1, w2)
    w_hi = jnp.where(first_lo, w2, w1)
    pidx = jnp.where(lo == 0, hi - 1, jnp.where(lo == 1, hi + 1, 5))
    cls = gsel * len(PAIR_LO) + pidx
    cls_ref[...] = jnp.broadcast_to(cls, (8, tm))
    rw_ref[...] = jnp.concatenate([w_lo, w_hi, jnp.zeros((6, tm), F32)], axis=0)
    _store_tiles(h2_ref, h2)


def _a2_kernel(bpp_ref, bpc_ref, bpn_ref, a_ref, x_ref, mod_ref, wpool_ref, bpool_ref, pscale_ref, wo_ref,
               gffn_ref, wrT_ref, br_ref, x1_ref, h2_ref, cls_ref, rw_ref, pb_ref, *, stride, npos, use_halo,
               ntile, nt):
    tm = x_ref.shape[0]
    halo = 8 * stride
    i = lax.rem(jnp.minimum(pl.program_id(0), ntile - 1), nt)
    cur = bpc_ref[...]
    if use_halo:
        pb_ref[0:halo, :] = jnp.where(i > 0, bpp_ref[...], 0.0)
        pb_ref[halo + tm:halo + tm + halo, :] = jnp.where(i < nt - 1, bpn_ref[...], 0.0)
    else:
        pb_ref[0:halo, :] = jnp.zeros((halo, D), F32)
        pb_ref[halo + tm:halo + tm + halo, :] = jnp.zeros((halo, D), F32)
    pb_ref[halo:halo + tm, :] = cur
    pos = (i * tm + lax.broadcasted_iota(I32, (tm, 1), 0)) // stride
    y = jnp.dot(a_ref[...], wo_ref[0:D, :], preferred_element_type=F32)
    for g, w in enumerate(POOL_WINDOWS):
        gsl = slice(g * POOL_GC, (g + 1) * POOL_GC)
        left = w // 2
        right = w - 1 - left
        tot = jnp.zeros((tm, POOL_GC), F32)
        for j in range(-left, right + 1):
            tot = tot + pb_ref[halo + j * stride:halo + j * stride + tm, gsl]
        cnt = (jnp.minimum(pos + right + 1, npos) - jnp.maximum(pos - left, 0)).astype(F32)
        pooled = tot / cnt - cur[:, gsl]
        pg = jnp.dot(pooled.astype(BF16), wpool_ref[g], preferred_element_type=F32) + bpool_ref[g:g + 1, :]
        pg = pg * pscale_ref[:, gsl]
        y = y + jnp.dot(pg.astype(BF16), wo_ref[D + g * POOL_GC:D + (g + 1) * POOL_GC, :],
                        preferred_element_type=F32)
    _tail(x_ref[...], y, mod_ref, gffn_ref, wrT_ref, br_ref, x1_ref, h2_ref, cls_ref, rw_ref)


def _a2_call(bp, a, x2d, mod, wpool, bpool, pscale, wo, gffn, wrT, br, h2_prev, *, seq, tm, stride, npos, use_halo,
             h2_rows, h2_row_off):
    n = x2d.shape[0]
    tps = seq // tm
    ntile = n // tm
    halo = 8 * stride
    off = h2_row_off // tm
    steps = ntile if h2_prev is not None else h2_rows // tm
    blk = lambda i: jnp.minimum(i, ntile - 1)
    row = lambda v: v.reshape(1, -1)
    const = lambda shape: pl.BlockSpec(shape, lambda i: (0,) * len(shape))
    in_specs = [
        pl.BlockSpec((tm, D), lambda i: (jnp.maximum(blk(i) - 1, 0), 0)),
        pl.BlockSpec((tm, D), lambda i: (blk(i), 0)),
        pl.BlockSpec((tm, D), lambda i: (jnp.minimum(blk(i) + 1, ntile - 1), 0)),
        pl.BlockSpec((tm, D), lambda i: (blk(i), 0)),
        pl.BlockSpec((tm, D), lambda i: (blk(i), 0)),
        pl.BlockSpec((1, 8, D), lambda i: ((blk(i) // tps) % mod.shape[0], 0, 0)),
        const((4, POOL_GC, POOL_GC)), const((4, POOL_GC)), const((1, D)), const((2 * D, D)),
        const((1, D)), const((N_EXPERTS, D)), const((N_EXPERTS, 1)),
    ]
    args = [bp, bp, bp, a, x2d, mod, wpool, bpool, row(pscale), wo, row(gffn), wrT, br.reshape(N_EXPERTS, 1)]
    aliases = {}
    body = functools.partial(_a2_kernel, stride=stride, npos=npos, use_halo=use_halo, ntile=ntile, nt=tps)
    if h2_prev is not None:
        in_specs.append(pl.BlockSpec(memory_space=pl.ANY))
        args.append(h2_prev)
        aliases = {len(args) - 1: 1}
        inner = body
        body = lambda *refs: inner(*refs[:13], *refs[14:])
    return pl.pallas_call(
        body,
        grid=(steps,),
        in_specs=in_specs,
        out_specs=[
            pl.BlockSpec((tm, D), lambda i: (blk(i), 0)),
            pl.BlockSpec((tm * 8, LANES), lambda i: (off + i, 0)),
            pl.BlockSpec((8, tm), lambda i: (0, blk(i))),
            pl.BlockSpec((8, tm), lambda i: (0, blk(i))),
        ],
        out_shape=[
            jax.ShapeDtypeStruct((n, D), F32),
            jax.ShapeDtypeStruct((h2_rows * 8, LANES), F32),
            jax.ShapeDtypeStruct((8, n), I32),
            jax.ShapeDtypeStruct((8, n), F32),
        ],
        scratch_shapes=[pltpu.VMEM((tm + 2 * halo, D), F32)],
        input_output_aliases=aliases,
        compiler_params=_cparams(("arbitrary",)),
        name="a2",
    )(*args)


def _moe_kernel(tlo_ref, thi_ref, tcnt_ref, idx_hbm, h2_hbm, rw_ref, w1lo, w3lo, w2lo, w1hi, w3hi, w2hi, y_hbm,
                idx_smem, xin, xb_ref, wt_ref, ybuf, sem_idx, sem_g, sem_s, *, tb, n_tiles):
    i = pl.program_id(0)
    last = n_tiles - 1
    unroll = 8

    def idx_copy(t):
        slot = lax.rem(t, IDX_RING)
        return pltpu.make_async_copy(idx_hbm.at[t], idx_smem.at[slot], sem_idx.at[slot])

    def tile_rows(j):
        return pl.ds(j * 8 if isinstance(j, int) else pl.multiple_of(j * 8, 8), 8)

    def gather_copy(row8, slot2, j):
        return pltpu.make_async_copy(h2_hbm.at[pl.ds(pl.multiple_of(row8, 8), 8), :],
                                     xin.at[slot2, tile_rows(j), :], sem_g.at[slot2])

    def gather_all(slot2):
        return pltpu.make_async_copy(h2_hbm.at[pl.ds(0, tb * 8), :], xin.at[slot2], sem_g.at[slot2])

    def scatter_copy(row8, slot2, j):
        return pltpu.make_async_copy(ybuf.at[slot2, tile_rows(j), :],
                                     y_hbm.at[pl.ds(pl.multiple_of(row8, 8), 8), :], sem_s.at[slot2])

    def rows(fn, count=tb):
        def body(jj, c):
            for u in range(unroll):
                fn(jj * unroll + u)
            return c
        full = count // unroll
        lax.fori_loop(0, full, body, 0)
        if not isinstance(count, int):
            def tail(j, c):
                fn(j)
                return c
            lax.fori_loop(full * unroll, count, tail, 0)

    def wait_scatter(t):
        s2 = lax.rem(t, 2)
        rows(lambda j: scatter_copy(0, s2, j).wait(), tcnt_ref[t])

    s2 = lax.rem(i, 2)

    @pl.when(i == 0)
    def _():
        for t in range(3):
            idx_copy(t).start()
        for t in range(2):
            idx_copy(t).wait()
            rows(lambda j: gather_copy(idx_smem[t, 0, j], t, j).start())

    @pl.when(i + 2 <= last)
    def _():
        idx_copy(i + 2).wait()

    @pl.when(i + 3 <= last)
    def _():
        idx_copy(i + 3).start()

    gather_all(s2).wait()

    @pl.when(i >= 2)
    def _():
        wait_scatter(i - 2)

    for s in range(D // LANES):
        xb_ref[:, s * LANES:(s + 1) * LANES] = xin[s2, pl.ds(s, tb, stride=8), :].astype(BF16)
    wt_ref[...] = jnp.zeros(wt_ref.shape, F32)
    wt_ref[0:8, :] = rw_ref[0]
    wcol = wt_ref[...].T
    sn = lax.rem(jnp.minimum(i + 2, last), IDX_RING)
    tp = jnp.maximum(i - 1, 0)
    cntp = jnp.where(i >= 1, tcnt_ref[tp], 0)
    sp = lax.rem(tp, IDX_RING)
    for j in range(tb):
        gather_copy(idx_smem[sn, 0, j], s2, j).start()
    for j in range(tb):
        @pl.when(j < cntp)
        def _():
            scatter_copy(idx_smem[sp, 0, j], 1 - s2, j).start()
    xb = xb_ref[...]

    def ffn(w1, w3, w2):
        h1 = jnp.dot(xb, w1[0], preferred_element_type=F32)
        h3 = jnp.dot(xb, w3[0], preferred_element_type=F32)
        act = (h1 * _sigmoid(h1)) * h3
        return jnp.dot(act.astype(BF16), w2[0], preferred_element_type=F32)

    y = wcol[:, 0:1] * ffn(w1lo, w3lo, w2lo) + wcol[:, 1:2] * ffn(w1hi, w3hi, w2hi)
    _store_tiles(ybuf.at[s2], y)

    @pl.when(i == last)
    def _():
        wait_scatter(i - 1)
        gather_all(0).wait()
        gather_all(1).wait()


def _moe_plan(cls, rw, n_tok, tb):
    n_tiles = -(-n_tok // tb) + N_CLASSES
    classes = jnp.arange(N_CLASSES, dtype=I32)
    _, order, wlo, whi = lax.sort((cls, jnp.arange(n_tok, dtype=I32), rw[0], rw[1]), num_keys=1, is_stable=True)
    counts = jnp.sum((cls[:, None] == classes[None, :]).astype(I32), axis=0)
    cstart = jnp.cumsum(counts) - counts
    tpc = (counts + tb - 1) // tb
    tend = jnp.cumsum(tpc)
    tstart = tend - tpc
    tile_ids = jnp.arange(n_tiles, dtype=I32)
    tile_cls = jnp.minimum(jnp.sum((tile_ids[:, None] >= tend[None, :]).astype(I32), axis=1), N_CLASSES - 1)
    first = (tile_ids - tstart[tile_cls]) * tb
    tcnt = jnp.clip(counts[tile_cls] - first, 0, tb)
    base = jnp.clip(cstart[tile_cls] + first, 0, n_tok)
    runs = lambda a: jax.vmap(lambda b: lax.dynamic_slice(jnp.pad(a, (0, tb)), (b,), (tb,)))(base)
    grp = tile_cls // len(PAIR_LO)
    pair = tile_cls % len(PAIR_LO)
    tlo = grp * EPG + jnp.asarray(PAIR_LO, I32)[pair]
    thi = grp * EPG + jnp.asarray(PAIR_HI, I32)[pair]
    rw_sorted = jnp.pad(jnp.stack([runs(wlo), runs(whi)], axis=1), ((0, 0), (0, 6), (0, 0)))
    return tlo.astype(I32), thi.astype(I32), tcnt.astype(I32), runs(order), rw_sorted, n_tiles


def _moe_call(h2t, cls, rw, w1, w3, w2, *, tb=MOE_TB):
    n_tok = cls.shape[0]
    assert n_tok % tb == 0
    tlo, thi, tcnt, src, rw_sorted, n_tiles = _moe_plan(cls, rw, n_tok, tb)
    idx = (src * 8).reshape(n_tiles, 1, tb)
    wspec_lo = pl.BlockSpec((1, D, D), lambda i, lo, hi, cnt: (lo[i], 0, 0))
    wspec_hi = pl.BlockSpec((1, D, D), lambda i, lo, hi, cnt: (hi[i], 0, 0))
    grid_spec = pltpu.PrefetchScalarGridSpec(
        num_scalar_prefetch=3,
        grid=(n_tiles,),
        in_specs=[pl.BlockSpec(memory_space=pl.ANY), pl.BlockSpec(memory_space=pl.ANY),
                  pl.BlockSpec((1, 8, tb), lambda i, lo, hi, cnt: (i, 0, 0)),
                  wspec_lo, wspec_lo, wspec_lo, wspec_hi, wspec_hi, wspec_hi],
        out_specs=pl.BlockSpec(memory_space=pl.ANY),
        scratch_shapes=[
            pltpu.SMEM((IDX_RING, 1, tb), I32),
            pltpu.VMEM((2, tb * 8, LANES), F32),
            pltpu.VMEM((tb, D), BF16),
            pltpu.VMEM((LANES, tb), F32),
            pltpu.VMEM((2, tb * 8, LANES), F32),
            pltpu.SemaphoreType.DMA((IDX_RING,)),
            pltpu.SemaphoreType.DMA((2,)),
            pltpu.SemaphoreType.DMA((2,)),
        ],
    )
    return pl.pallas_call(
        functools.partial(_moe_kernel, tb=tb, n_tiles=n_tiles),
        grid_spec=grid_spec,
        out_shape=jax.ShapeDtypeStruct((n_tok * 8, LANES), F32),
        compiler_params=_cparams(("arbitrary",)),
        name="moe",
    )(tlo, thi, tcnt, idx, h2t, rw_sorted, w1, w3, w2, w1, w3, w2)


def _m1_kernel(xp_ref, xc_ref, xn_ref, yp_ref, yc_ref, yn_ref, modp_ref, modc_ref, g_ref, win_ref, wconv_ref,
               bconv_ref, bdq_ref, bdk_ref, bdv_ref, wgT_ref, bg_ref,
               x2_ref, q_ref, k_ref, v_ref, o_ref, gT_ref, xm_ref):
    tm = xc_ref.shape[0]
    i = pl.program_id(1)
    nt = pl.num_programs(1)
    xe = jnp.concatenate([xp_ref[...], xc_ref[...], xn_ref[...]], axis=0)
    ye = jnp.concatenate([_load_tiles(yp_ref, 8), _load_tiles(yc_ref, tm), _load_tiles(yn_ref, 8)], axis=0)
    xe = xe + modp_ref[0, 5:6, :] * ye
    x2_ref[...] = xe[8:8 + tm]
    h = _rms_mod(xe, g_ref[...], modc_ref[0, 0:1, :], modc_ref[0, 1:2, :]).astype(BF16)
    xm = jnp.dot(h, win_ref[:, 0:E], preferred_element_type=F32)
    z = jnp.dot(h, win_ref[:, E:2 * E], preferred_element_type=F32)
    o_ref[...] = _sigmoid(z[8:8 + tm]).astype(BF16)
    rid = lax.broadcasted_iota(I32, (tm + 16, 1), 0)
    inside = jnp.logical_and(jnp.logical_or(rid >= 8, i > 0), jnp.logical_or(rid < tm + 8, i < nt - 1))
    xm = jnp.where(inside, xm, 0.0)
    xm_ref[...] = xm
    acc = jnp.zeros((tm, E), F32) + bconv_ref[...]
    for kk in range(SHORT_K):
        acc = acc + wconv_ref[kk:kk + 1, :] * xm_ref[6 + kk:6 + kk + tm, :]
    xcb = (acc * _sigmoid(acc)).astype(BF16)
    xmb = xm[8:8 + tm].astype(BF16)
    nt_dims = (((1,), (1,)), ((), ()))
    gT = jnp.zeros((4 * NH, tm), F32) + bg_ref[...]
    kscale = DH ** -0.5
    for j in range(E // MXU_DIM):
        sl = slice(j * MXU_DIM, (j + 1) * MXU_DIM)
        qj = jnp.dot(xcb[:, sl], bdq_ref[j], preferred_element_type=F32)
        kj = jnp.dot(xcb[:, sl], bdk_ref[j], preferred_element_type=F32)
        vj = jnp.dot(xmb[:, sl], bdv_ref[j], preferred_element_type=F32)
        qb, kb, vb = qj.astype(BF16), kj.astype(BF16), vj.astype(BF16)
        gT = gT + lax.dot_general(wgT_ref[:, sl], qb, nt_dims, preferred_element_type=F32)
        gT = gT + lax.dot_general(wgT_ref[:, E + j * MXU_DIM:E + (j + 1) * MXU_DIM], kb, nt_dims,
                                  preferred_element_type=F32)
        gT = gT + lax.dot_general(wgT_ref[:, 2 * E + j * MXU_DIM:2 * E + (j + 1) * MXU_DIM], vb, nt_dims,
                                  preferred_element_type=F32)
        q_ref[:, sl] = qb
        k_ref[:, sl] = (kj * kscale).astype(BF16)
        v_ref[:, sl] = vb
    gT_ref[...] = gT


def _m1_call(xa, ya, modp, modc, g, win, wconv, bconv, bdq, bdk, bdv, wgT, bg, *, nb, seq, tm, y_row_off):
    n = nb * seq
    tps = seq // tm
    hb = tm // 8
    nb8 = n // 8
    yo = y_row_off // tm
    yo8 = y_row_off // 8
    blk = lambda b, i: b * tps + i
    const = lambda shape: pl.BlockSpec(shape, lambda b, i: (0,) * len(shape))
    prev8 = lambda b, i: jnp.maximum(blk(b, i) * hb - 1, 0)
    next8 = lambda b, i: jnp.minimum((blk(b, i) + 1) * hb, nb8 - 1)
    in_specs = [
        pl.BlockSpec((8, D), lambda b, i: (prev8(b, i), 0)),
        pl.BlockSpec((tm, D), lambda b, i: (blk(b, i), 0)),
        pl.BlockSpec((8, D), lambda b, i: (next8(b, i), 0)),
        pl.BlockSpec((8 * 8, LANES), lambda b, i: (yo8 + prev8(b, i), 0)),
        pl.BlockSpec((tm * 8, LANES), lambda b, i: (yo + blk(b, i), 0)),
        pl.BlockSpec((8 * 8, LANES), lambda b, i: (yo8 + next8(b, i), 0)),
        pl.BlockSpec((1, 8, D), lambda b, i: (b % modp.shape[0], 0, 0)),
        pl.BlockSpec((1, 8, D), lambda b, i: (b % modc.shape[0], 0, 0)),
        const((1, D)), const((D, 2 * E)), const((SHORT_K, E)), const((1, E)),
        const((E // MXU_DIM, MXU_DIM, MXU_DIM)), const((E // MXU_DIM, MXU_DIM, MXU_DIM)),
        const((E // MXU_DIM, MXU_DIM, MXU_DIM)), const((4 * NH, 3 * E)), const((4 * NH, 1)),
    ]
    tok = lambda w, dt: jax.ShapeDtypeStruct((n, w), dt)
    tspec = lambda w: pl.BlockSpec((tm, w), lambda b, i: (blk(b, i), 0))
    return pl.pallas_call(
        _m1_kernel,
        grid=(nb, tps),
        in_specs=in_specs,
        out_specs=[tspec(D), tspec(E), tspec(E), tspec(E), tspec(E),
                   pl.BlockSpec((4 * NH, tm), lambda b, i: (0, blk(b, i)))],
        out_shape=[tok(D, F32), tok(E, BF16), tok(E, BF16), tok(E, BF16), tok(E, BF16),
                   jax.ShapeDtypeStruct((4 * NH, n), F32)],
        scratch_shapes=[pltpu.VMEM((tm + 16, E), F32)],
        compiler_params=_cparams(("parallel", "arbitrary")),
        name="m1",
    )(xa, xa, xa, ya, ya, ya, modp, modc, g.reshape(1, D), win, wconv, bconv.reshape(1, E), bdq, bdk, bdv, wgT,
      bg.reshape(4 * NH, 1))


def _log_sigmoid(x):
    return jnp.minimum(x, 0.0) - jnp.log1p(jnp.exp(-jnp.abs(x)))


def _scan_masks(t, rev):
    r = lax.broadcasted_iota(I32, (t, t), 0)
    c = lax.broadcasted_iota(I32, (t, t), 1)
    tri = (c >= r) if rev else (c <= r)
    tri_t = (r >= c) if rev else (r <= c)
    return tri, tri_t, c == r


def _scan_step(q, k, v, i_row, f_row, c_ref, n_ref, m_ref, masks):
    t = q.shape[0]
    tri, tri_t, eye = masks
    lf = _log_sigmoid(f_row)
    lf_b = jnp.broadcast_to(lf, (t, t))
    b_col = jnp.sum(jnp.where(tri, lf_b, 0.0), axis=1, keepdims=True)
    lf_col = jnp.sum(jnp.where(eye, lf_b, 0.0), axis=1, keepdims=True)
    i_col = jnp.sum(jnp.where(eye, jnp.broadcast_to(i_row, (t, t)), 0.0), axis=1, keepdims=True)
    b_row = jnp.sum(jnp.where(tri_t, jnp.broadcast_to(lf_col, (t, t)), 0.0), axis=0, keepdims=True)
    m = m_ref[0:1, 0:1]
    dmat = jnp.where(tri, b_col - b_row + i_row, -jnp.inf)
    inter = b_col + m
    m_t = jnp.maximum(inter, jnp.max(dmat, axis=1, keepdims=True))
    wmat = jnp.exp(dmat - m_t)
    w_inter = jnp.exp(inter - m_t)
    s = lax.dot_general(q, k, (((1,), (1,)), ((), ())), preferred_element_type=F32) * wmat
    cb = c_ref[...].astype(BF16)
    num = jnp.dot(s.astype(BF16), v, preferred_element_type=F32) \
        + w_inter * jnp.dot(q, cb, preferred_element_type=F32)
    qn = jnp.sum(q.astype(F32) * n_ref[...], axis=1, keepdims=True)
    den = jnp.sum(s, axis=1, keepdims=True) + w_inter * qn
    h = num / jnp.maximum(jnp.abs(den), jnp.exp(-m_t))
    b_end = jnp.sum(lf, axis=1, keepdims=True)
    dlog = b_end - b_col + i_col
    m_new = jnp.maximum(b_end + m, jnp.max(dlog, axis=0, keepdims=True))
    wk = jnp.exp(dlog - m_new)
    decay = jnp.exp(b_end + m - m_new)
    kf = k.astype(F32) * wk
    c_ref[...] = decay * c_ref[...] + lax.dot_general(kf.astype(BF16), v, (((0,), (0,)), ((), ())),
                                                      preferred_element_type=F32)
    n_ref[...] = decay * n_ref[...] + jnp.sum(kf, axis=0, keepdims=True)
    m_ref[...] = jnp.broadcast_to(m_new, m_ref.shape)
    return h


def _scan_kernel(*refs, rev, final):
    qc, kc, vc, ic, fc, ql, kl, vl, il, fl = refs[:10]
    if final:
        hf_ref, o_ref, hng_ref, out_ref, c_ref, n_ref, m_ref = refs[10:]
    else:
        out_ref, c_ref, n_ref, m_ref = refs[10:]
    j = pl.program_id(2)
    masks = _scan_masks(CHUNK_T, rev)
    heads = [slice(hh * DH, (hh + 1) * DH) for hh in range(SCAN_HEADS)]

    def step(qr, kr, vr, ir, fr, hh):
        hs = heads[hh]
        return _scan_step(qr[0, :, hs], kr[0, :, hs], vr[0, :, hs], ir[hh], fr[hh],
                          c_ref.at[hh], n_ref.at[hh], m_ref.at[hh], masks)

    @pl.when(j == 0)
    def _():
        c_ref[...] = jnp.zeros(c_ref.shape, F32)
        n_ref[...] = jnp.zeros(n_ref.shape, F32)
        m_ref[...] = jnp.zeros(m_ref.shape, F32)
        for hh in range(SCAN_HEADS):
            step(qc, kc, vc, ic, fc, hh)

    @pl.when(j > 0)
    def _():
        for hh in range(SCAN_HEADS):
            h = step(ql, kl, vl, il, fl, hh)
            if final:
                hs = h + hf_ref[0, :, heads[hh]]
                mu = jnp.mean(hs, axis=-1, keepdims=True)
                hc = hs - mu
                var = jnp.mean(hc * hc, axis=-1, keepdims=True)
                hn = hc * lax.rsqrt(var + EPS) * hng_ref[:, heads[hh]]
                out_ref[0, :, heads[hh]] = (o_ref[0, :, heads[hh]] * hn.astype(BF16)).astype(BF16)
            else:
                out_ref[0, :, heads[hh]] = h


def _scan_call(qc, kc, vc, gc, ql, kl, vl, gl, hf, o, hng, *, nb, seq, rev):
    final = rev
    t = CHUNK_T
    ncl = seq // t
    if rev:
        cidx = lambda j: ncl - jnp.maximum(j, 1)
        irow, frow = 2 * NH, 3 * NH
    else:
        cidx = lambda j: jnp.maximum(j - 1, 0)
        irow, frow = 0, NH
    hp = SCAN_HEADS
    wid = hp * DH
    cspec = pl.BlockSpec((1, t, wid), lambda b, h, j: (b, 0, h))
    lspec = pl.BlockSpec((1, t, wid), lambda b, h, j: (b, cidx(j), h))
    gcs = lambda r0: pl.BlockSpec((hp, 1, t), lambda b, h, j: (r0 // hp + h, 0, b))
    gls = lambda r0: pl.BlockSpec((hp, 1, t), lambda b, h, j: (r0 // hp + h, 0, b * ncl + cidx(j)))
    in_specs = [cspec, cspec, cspec, gcs(irow), gcs(frow), lspec, lspec, lspec, gls(irow), gls(frow)]
    args = [qc, kc, vc, gc, gc, ql, kl, vl, gl, gl]
    if final:
        in_specs += [lspec, lspec, pl.BlockSpec((1, wid), lambda b, h, j: (0, h))]
        args += [hf, o, hng.reshape(1, E)]
    return pl.pallas_call(
        functools.partial(_scan_kernel, rev=rev, final=final),
        grid=(nb, NH // hp, ncl + 1),
        in_specs=in_specs,
        out_specs=lspec,
        out_shape=jax.ShapeDtypeStruct((nb, seq, E), BF16 if final else F32),
        scratch_shapes=[pltpu.VMEM((hp, DH, DH), F32), pltpu.VMEM((hp, 1, DH), F32),
                        pltpu.VMEM((hp, 8, LANES), F32)],
        compiler_params=_cparams(("parallel", "parallel", "arbitrary")),
        name="scan_bwd" if rev else "scan_fwd",
    )(*args)


def _m3_kernel(hg_ref, x_ref, mod_ref, wo_ref, gffn_ref, wrT_ref, br_ref, x1_ref, h2_ref, cls_ref, rw_ref):
    y = jnp.dot(hg_ref[...], wo_ref[...], preferred_element_type=F32)
    _tail(x_ref[...], y, mod_ref, gffn_ref, wrT_ref, br_ref, x1_ref, h2_ref, cls_ref, rw_ref)


def _m3_call(hg, x2d, mod, wo, gffn, wrT, br, *, seq, tm):
    n = x2d.shape[0]
    tps = seq // tm
    const = lambda shape: pl.BlockSpec(shape, lambda i: (0,) * len(shape))
    return pl.pallas_call(
        _m3_kernel,
        grid=(n // tm,),
        in_specs=[
            pl.BlockSpec((tm, E), lambda i: (i, 0)),
            pl.BlockSpec((tm, D), lambda i: (i, 0)),
            pl.BlockSpec((1, 8, D), lambda i: (i // tps, 0, 0)),
            const((E, D)), const((1, D)), const((N_EXPERTS, D)), const((N_EXPERTS, 1)),
        ],
        out_specs=[
            pl.BlockSpec((tm, D), lambda i: (i, 0)),
            pl.BlockSpec((tm * 8, LANES), lambda i: (i, 0)),
            pl.BlockSpec((8, tm), lambda i: (0, i)),
            pl.BlockSpec((8, tm), lambda i: (0, i)),
        ],
        out_shape=[
            jax.ShapeDtypeStruct((n, D), F32),
            jax.ShapeDtypeStruct((n * 8, LANES), F32),
            jax.ShapeDtypeStruct((8, n), I32),
            jax.ShapeDtypeStruct((8, n), F32),
        ],
        compiler_params=_cparams(("parallel",)),
        name="m3",
    )(hg, x2d, mod, wo, gffn.reshape(1, D), wrT, br.reshape(N_EXPERTS, 1))


def _final_kernel(x_ref, y_ref, mod_ref, g_ref, o_ref):
    x = x_ref[...] + mod_ref[0, 5:6, :] * _load_tiles(y_ref, x_ref.shape[0])
    ms = jnp.mean(x * x, axis=-1, keepdims=True)
    o_ref[...] = x * lax.rsqrt(ms + EPS) * g_ref[...]


def _final_call(x2d, y, mod, g, *, seq, tm):
    n = x2d.shape[0]
    tps = seq // tm
    return pl.pallas_call(
        _final_kernel,
        grid=(n // tm,),
        in_specs=[
            pl.BlockSpec((tm, D), lambda i: (i, 0)),
            pl.BlockSpec((tm * 8, LANES), lambda i: (i, 0)),
            pl.BlockSpec((1, 8, D), lambda i: (i // tps, 0, 0)),
            pl.BlockSpec((1, D), lambda i: (0, 0)),
        ],
        out_specs=pl.BlockSpec((tm, D), lambda i: (i, 0)),
        out_shape=jax.ShapeDtypeStruct((n, D), F32),
        compiler_params=_cparams(("parallel",)),
        name="final",
    )(x2d, y, mod, g.reshape(1, D))


def _blockdiag_tiles(w):
    per = MXU_DIM // 4
    w = w.reshape(E // MXU_DIM, per, 4, 4)
    eye = jnp.eye(per, dtype=w.dtype)
    t = w[:, :, :, None, :] * eye[None, :, None, :, None]
    return t.reshape(E // MXU_DIM, MXU_DIM, MXU_DIM)


def kernel(x, c, ctx, c_ctx, w_ada, b_ada, g_mix, g_ffn, g_final, w_in_ab, w_dw_a, b_dw_a, ln_g_a, ln_b_a, w_pool,
           b_pool, pool_scale, w_out_ab, w_in_c, w_conv_c, b_conv_c, w_q, w_k, w_v, w_gates, b_gates, hn_g, w_out_c,
           w_router, b_router, w1, w3, w2):
    nb, seq, _ = x.shape
    ctx_len = ctx.shape[1]
    assert ctx_len == CHUNK_T and seq % TM == 0 and seq % GRID_W == 0 and nb < 16
    n_lat = nb * seq
    n_ctx = nb * ctx_len
    x_lat = x.reshape(n_lat, D)
    x_ctx = ctx.reshape(n_ctx, D)

    s16 = jnp.zeros((16, D), F32).at[:nb].set(c).at[nb].set(c_ctx)
    mod = _mod_call(s16, w_ada, b_ada)
    pad8 = lambda m: jnp.pad(m.reshape(-1, 6, D), ((0, 0), (0, 2), (0, 0)))
    mod_lat = [pad8(mod[l, :nb]) for l in range(2)]
    mod_ctx = [pad8(mod[l, nb:nb + 1]) for l in range(2)]

    wrT = w_router.T
    bf = lambda w: w.astype(BF16)

    prm1 = (g_mix[0], bf(w_in_ab[0]), w_dw_a[0], b_dw_a[0], ln_g_a[0], ln_b_a[0])
    a_lat, bp_lat = _a1_call(x_lat, mod_lat[0], *prm1, seq=seq, tm=TM, seg=GRID_W)
    a_ctx, bp_ctx = _a1_call(x_ctx, mod_ctx[0], *prm1, seq=ctx_len, tm=ctx_len, seg=ctx_len)
    prm2 = (bf(w_pool[0]), b_pool[0], pool_scale[0], bf(w_out_ab[0]), g_ffn[0], wrT, b_router)
    n_tot = n_lat + n_ctx
    x1_lat, h2t, cls_lat, rw_lat = _a2_call(bp_lat, a_lat, x_lat, mod_lat[0], *prm2, None, seq=seq, tm=TM,
                                            stride=GRID_W, npos=seq // GRID_W, use_halo=True, h2_rows=n_tot,
                                            h2_row_off=0)
    x1_ctx, h2t, cls_ctx, rw_ctx = _a2_call(bp_ctx, a_ctx, x_ctx, mod_ctx[0], *prm2, h2t, seq=ctx_len, tm=ctx_len,
                                            stride=1, npos=ctx_len, use_halo=False, h2_rows=n_tot,
                                            h2_row_off=n_lat)
    cls0 = jnp.concatenate([cls_lat[0], cls_ctx[0]])
    rw0 = jnp.concatenate([rw_lat, rw_ctx], axis=1)
    y0 = _moe_call(h2t, cls0, rw0, bf(w1[0]), bf(w3[0]), bf(w2[0]))

    wgT = bf(w_gates[0].T)
    prm3 = (g_mix[1], bf(w_in_c[0]), w_conv_c[0], b_conv_c[0], bf(_blockdiag_tiles(w_q[0])),
            bf(_blockdiag_tiles(w_k[0])), bf(_blockdiag_tiles(w_v[0])), wgT, b_gates[0])
    x2_lat, ql, kl, vl, ol, gl = _m1_call(x1_lat, y0, mod_lat[0], mod_lat[1], *prm3, nb=nb, seq=seq, tm=TM1,
                                          y_row_off=0)
    _, qc, kc, vc, _, gc = _m1_call(x1_ctx, y0, mod_ctx[0], mod_ctx[1], *prm3, nb=nb, seq=ctx_len, tm=ctx_len,
                                    y_row_off=n_lat)
    r3 = lambda t, length: t.reshape(nb, length, E)
    ql, kl, vl, ol = (r3(t, seq) for t in (ql, kl, vl, ol))
    qc, kc, vc = (r3(t, ctx_len) for t in (qc, kc, vc))
    gl = gl.reshape(4 * NH, 1, n_lat)
    gc = gc.reshape(4 * NH, 1, n_ctx)
    hf = _scan_call(qc, kc, vc, gc, ql, kl, vl, gl, None, None, None, nb=nb, seq=seq, rev=False)
    hg = _scan_call(qc, kc, vc, gc, ql, kl, vl, gl, hf, ol, hn_g[0], nb=nb, seq=seq, rev=True)
    x3_lat, h2t1, cls1, rw1 = _m3_call(hg.reshape(n_lat, E), x2_lat, mod_lat[1], bf(w_out_c[0]), g_ffn[1], wrT,
                                       b_router, seq=seq, tm=TM)
    y1 = _moe_call(h2t1, cls1[0], rw1, bf(w1[1]), bf(w3[1]), bf(w2[1]))
    out = _final_call(x3_lat, y1, mod_lat[1], g_final, seq=seq, tm=TM)
    return out.reshape(nb, seq, D)
```

```python
import functools

import jax
import jax.numpy as jnp
from jax import lax
from jax.experimental import pallas as pl
from jax.experimental.pallas import tpu as pltpu

F32 = jnp.float32
BF16 = jnp.bfloat16
I32 = jnp.int32
HIGHEST = lax.Precision.HIGHEST

D = 1024
EPS = 1e-6
GRID_W = 64
CONV_K = 31
POOL_WINDOWS = (2, 4, 8, 16)
POOL_GC = 256
E = 2 * D
NH = 4
DH = E // NH
SHORT_K = 4
N_EXPERTS = 16
EPG = 4
N_CLASSES = 24
PAIR_LO = (0, 0, 0, 1, 1, 2)
PAIR_HI = (1, 2, 3, 2, 3, 3)

LANES = 128
MXU_DIM = 256
TM = 512
TM1 = 256
CHUNK_T = 256
SCAN_HEADS = 4
MOE_TB = 256
IDX_RING = 8
VMEM_LIMIT = 56 * 1024 * 1024


def _cparams(sem):
    return pltpu.CompilerParams(dimension_semantics=sem, vmem_limit_bytes=VMEM_LIMIT)


def _sigmoid(x):
    return jax.nn.sigmoid(x)


def _rms_mod(x, g, shift, scale):
    ms = jnp.mean(x * x, axis=-1, keepdims=True)
    return (x * lax.rsqrt(ms + EPS) * g) * (1.0 + scale) + shift


def _mod_kernel(s_ref, w_ref, b_ref, o_ref):
    s = s_ref[...]
    s = s * _sigmoid(s)
    o_ref[0] = jnp.dot(s, w_ref[0], preferred_element_type=F32, precision=HIGHEST) + b_ref[0]


def _mod_call(s16, w_ada, b_ada):
    depth = w_ada.shape[0]
    tn = 1536
    return pl.pallas_call(
        _mod_kernel,
        grid=(depth, 6 * D // tn),
        in_specs=[
            pl.BlockSpec((16, D), lambda l, n: (0, 0)),
            pl.BlockSpec((1, D, tn), lambda l, n: (l, 0, n)),
            pl.BlockSpec((1, 1, tn), lambda l, n: (l, 0, n)),
        ],
        out_specs=pl.BlockSpec((1, 16, tn), lambda l, n: (l, 0, n)),
        out_shape=jax.ShapeDtypeStruct((depth, 16, 6 * D), F32),
        compiler_params=_cparams(("parallel", "parallel")),
        name="mod",
    )(s16, w_ada, b_ada.reshape(depth, 1, 6 * D))


def _a1_kernel(x_ref, mod_ref, g_ref, win_ref, wdw_ref, bdw_ref, lng_ref, lnb_ref,
               a_ref, bp_ref, pad_ref, conv_ref, shf_ref, *, seg):
    tm = x_ref.shape[0]
    nseg = tm // seg
    h = _rms_mod(x_ref[...], g_ref[...], mod_ref[0, 0:1, :], mod_ref[0, 1:2, :])
    u = jnp.dot(h.astype(BF16), win_ref[...], preferred_element_type=F32)
    a = u[:, :D] * _sigmoid(u[:, D:2 * D])
    bp_ref[...] = u[:, 2 * D:]
    zeros16 = jnp.zeros((nseg, 16, D), F32)
    pad_ref[:, 0:16, :] = zeros16
    pad_ref[:, seg + 16:seg + 32, :] = zeros16
    pad_ref[:, 16:16 + seg, :] = a.reshape(nseg, seg, D)

    def seg_body(s, carry):
        for cs in range(D // LANES):
            sl = slice(cs * LANES, (cs + 1) * LANES)
            acc = jnp.zeros((seg, LANES), F32)
            for r in range(8):
                starts = [st for st in range(r, CONV_K + 1, 8) if st >= 1]
                lo = starts[0]
                nrow = starts[-1] - lo + seg
                if r:
                    shf_ref[r, 0:nrow, :] = pad_ref[s, pl.ds(lo, nrow), sl]
                    src = lambda st: shf_ref[r, st - lo:st - lo + seg, :]
                else:
                    src = lambda st: pad_ref[s, pl.ds(st, seg), sl]
                for st in starts:
                    acc = acc + wdw_ref[st - 1:st, sl] * src(st)
            conv_ref[pl.ds(pl.multiple_of(s * seg, seg), seg), sl] = acc + bdw_ref[:, sl]
        return carry

    lax.fori_loop(0, nseg, seg_body, 0)
    cv = conv_ref[...]
    mu = jnp.mean(cv, axis=-1, keepdims=True)
    cc = cv - mu
    var = jnp.mean(cc * cc, axis=-1, keepdims=True)
    y = cc * lax.rsqrt(var + EPS) * lng_ref[...] + lnb_ref[...]
    a_ref[...] = (y * _sigmoid(y)).astype(BF16)


def _a1_call(x2d, mod, g, win, wdw, bdw, lng, lnb, *, seq, tm, seg):
    n = x2d.shape[0]
    tps = seq // tm
    row = lambda v: v.reshape(1, -1)
    const = lambda shape: pl.BlockSpec(shape, lambda i: (0,) * len(shape))
    return pl.pallas_call(
        functools.partial(_a1_kernel, seg=seg),
        grid=(n // tm,),
        in_specs=[
            pl.BlockSpec((tm, D), lambda i: (i, 0)),
            pl.BlockSpec((1, 8, D), lambda i: ((i // tps) % mod.shape[0], 0, 0)),
            const((1, D)), const((D, 3 * D)), const((CONV_K, D)), const((1, D)), const((1, D)), const((1, D)),
        ],
        out_specs=[pl.BlockSpec((tm, D), lambda i: (i, 0)), pl.BlockSpec((tm, D), lambda i: (i, 0))],
        out_shape=[jax.ShapeDtypeStruct((n, D), BF16), jax.ShapeDtypeStruct((n, D), F32)],
        scratch_shapes=[pltpu.VMEM((tm // seg, seg + 32, D), F32), pltpu.VMEM((tm, D), F32),
                        pltpu.VMEM((8, seg + 24, LANES), F32)],
        compiler_params=_cparams(("parallel",)),
        name="a1",
    )(x2d, mod, row(g), win, wdw, row(bdw), row(lng), row(lnb))


def _store_tiles(ref, val):
    n = val.shape[0]
    for s in range(D // LANES):
        ref[pl.ds(s, n, stride=8), :] = val[:, s * LANES:(s + 1) * LANES]


def _load_tiles(ref, n):
    return jnp.concatenate([ref[pl.ds(s, n, stride=8), :] for s in range(D // LANES)], axis=-1)


def _tail(x, y, mod_ref, gffn_ref, wrT_ref, br_ref, x1_ref, h2_ref, cls_ref, rw_ref):
    tm = x.shape[0]
    x1 = x + mod_ref[0, 2:3, :] * y
    x1_ref[...] = x1
    h2 = _rms_mod(x1, gffn_ref[...], mod_ref[0, 3:4, :], mod_ref[0, 4:5, :])
    logits = lax.dot_general(wrT_ref[...], h2, (((1,), (1,)), ((), ())),
                             precision=HIGHEST, preferred_element_type=F32)
    sc = _sigmoid(logits)
    bi = sc + br_ref[...]
    gs = []
    for g in range(N_EXPERTS // EPG):
        v = [bi[EPG * g + j:EPG * g + j + 1, :] for j in range(EPG)]
        best = v[0] + v[1]
        for (p, q) in ((0, 2), (0, 3), (1, 2), (1, 3), (2, 3)):
            best = jnp.maximum(best, v[p] + v[q])
        gs.append(best)
    gsel = jnp.zeros((1, tm), I32)
    gbest = gs[0]
    for g in range(1, N_EXPERTS // EPG):
        better = gs[g] > gbest
        gsel = jnp.where(better, g, gsel)
        gbest = jnp.where(better, gs[g], gbest)

    def pick(arr, j):
        out = arr[j:j + 1, :]
        for g in range(1, N_EXPERTS // EPG):
            out = jnp.where(gsel == g, arr[EPG * g + j:EPG * g + j + 1, :], out)
        return out

    cand = [pick(bi, j) for j in range(EPG)]
    scs = [pick(sc, j) for j in range(EPG)]
    i1 = jnp.zeros((1, tm), I32)
    b1 = cand[0]
    for j in range(1, EPG):
        better = cand[j] > b1
        i1 = jnp.where(better, j, i1)
        b1 = jnp.where(better, cand[j], b1)
    neg = jnp.full((1, tm), -jnp.inf, F32)
    i2 = jnp.zeros((1, tm), I32)
    b2 = neg
    for j in range(EPG):
        cj = jnp.where(i1 == j, neg, cand[j])
        better = cj > b2
        i2 = jnp.where(better, j, i2)
        b2 = jnp.where(better, cj, b2)
    w1 = jnp.zeros((1, tm), F32)
    w2 = jnp.zeros((1, tm), F32)
    for j in range(EPG):
        w1 = jnp.where(i1 == j, scs[j], w1)
        w2 = jnp.where(i2 == j, scs[j], w2)
    den = w1 + w2
    w1 = w1 / den
    w2 = w2 / den
    first_lo = i1 < i2
    lo = jnp.minimum(i1, i2)
    hi = jnp.maximum(i1, i2)
    w_lo = jnp.where(first_lo, w1, w2)
    w_hi = jnp.where(first_lo, w2, w1)
    pidx = jnp.where(lo == 0, hi - 1, jnp.where(lo == 1, hi + 1, 5))
    cls = gsel * len(PAIR_LO) + pidx
    cls_ref[...] = jnp.broadcast_to(cls, (8, tm))
    rw_ref[...] = jnp.concatenate([w_lo, w_hi, jnp.zeros((6, tm), F32)], axis=0)
    _store_tiles(h2_ref, h2)


def _a2_kernel(bpp_ref, bpc_ref, bpn_ref, a_ref, x_ref, mod_ref, wpool_ref, bpool_ref, pscale_ref, wo_ref,
               gffn_ref, wrT_ref, br_ref, x1_ref, h2_ref, cls_ref, rw_ref, pb_ref, *, stride, npos, use_halo,
               ntile, nt):
    tm = x_ref.shape[0]
    halo = 8 * stride
    i = lax.rem(jnp.minimum(pl.program_id(0), ntile - 1), nt)
    cur = bpc_ref[...]
    if use_halo:
        pb_ref[0:halo, :] = jnp.where(i > 0, bpp_ref[...], 0.0)
        pb_ref[halo + tm:halo + tm + halo, :] = jnp.where(i < nt - 1, bpn_ref[...], 0.0)
    else:
        pb_ref[0:halo, :] = jnp.zeros((halo, D), F32)
        pb_ref[halo + tm:halo + tm + halo, :] = jnp.zeros((halo, D), F32)
    pb_ref[halo:halo + tm, :] = cur
    pos = (i * tm + lax.broadcasted_iota(I32, (tm, 1), 0)) // stride
    y = jnp.dot(a_ref[...], wo_ref[0:D, :], preferred_element_type=F32)
    for g, w in enumerate(POOL_WINDOWS):
        gsl = slice(g * POOL_GC, (g + 1) * POOL_GC)
        left = w // 2
        right = w - 1 - left
        tot = jnp.zeros((tm, POOL_GC), F32)
        for j in range(-left, right + 1):
            tot = tot + pb_ref[halo + j * stride:halo + j * stride + tm, gsl]
        cnt = (jnp.minimum(pos + right + 1, npos) - jnp.maximum(pos - left, 0)).astype(F32)
        pooled = tot / cnt - cur[:, gsl]
        pg = jnp.dot(pooled.astype(BF16), wpool_ref[g], preferred_element_type=F32) + bpool_ref[g:g + 1, :]
        pg = pg * pscale_ref[:, gsl]
        y = y + jnp.dot(pg.astype(BF16), wo_ref[D + g * POOL_GC:D + (g + 1) * POOL_GC, :],
                        preferred_element_type=F32)
    _tail(x_ref[...], y, mod_ref, gffn_ref, wrT_ref, br_ref, x1_ref, h2_ref, cls_ref, rw_ref)


def _a2_call(bp, a, x2d, mod, wpool, bpool, pscale, wo, gffn, wrT, br, h2_prev, *, seq, tm, stride, npos, use_halo,
             h2_rows, h2_row_off):
    n = x2d.shape[0]
    tps = seq // tm
    ntile = n // tm
    halo = 8 * stride
    off = h2_row_off // tm
    steps = ntile if h2_prev is not None else h2_rows // tm
    blk = lambda i: jnp.minimum(i, ntile - 1)
    row = lambda v: v.reshape(1, -1)
    const = lambda shape: pl.BlockSpec(shape, lambda i: (0,) * len(shape))
    in_specs = [
        pl.BlockSpec((tm, D), lambda i: (jnp.maximum(blk(i) - 1, 0), 0)),
        pl.BlockSpec((tm, D), lambda i: (blk(i), 0)),
        pl.BlockSpec((tm, D), lambda i: (jnp.minimum(blk(i) + 1, ntile - 1), 0)),
        pl.BlockSpec((tm, D), lambda i: (blk(i), 0)),
        pl.BlockSpec((tm, D), lambda i: (blk(i), 0)),
        pl.BlockSpec((1, 8, D), lambda i: ((blk(i) // tps) % mod.shape[0], 0, 0)),
        const((4, POOL_GC, POOL_GC)), const((4, POOL_GC)), const((1, D)), const((2 * D, D)),
        const((1, D)), const((N_EXPERTS, D)), const((N_EXPERTS, 1)),
    ]
    args = [bp, bp, bp, a, x2d, mod, wpool, bpool, row(pscale), wo, row(gffn), wrT, br.reshape(N_EXPERTS, 1)]
    aliases = {}
    body = functools.partial(_a2_kernel, stride=stride, npos=npos, use_halo=use_halo, ntile=ntile, nt=tps)
    if h2_prev is not None:
        in_specs.append(pl.BlockSpec(memory_space=pl.ANY))
        args.append(h2_prev)
        aliases = {len(args) - 1: 1}
        inner = body
        body = lambda *refs: inner(*refs[:13], *refs[14:])
    return pl.pallas_call(
        body,
        grid=(steps,),
        in_specs=in_specs,
        out_specs=[
            pl.BlockSpec((tm, D), lambda i: (blk(i), 0)),
            pl.BlockSpec((tm * 8, LANES), lambda i: (off + i, 0)),
            pl.BlockSpec((8, tm), lambda i: (0, blk(i))),
            pl.BlockSpec((8, tm), lambda i: (0, blk(i))),
        ],
        out_shape=[
            jax.ShapeDtypeStruct((n, D), F32),
            jax.ShapeDtypeStruct((h2_rows * 8, LANES), F32),
            jax.ShapeDtypeStruct((8, n), I32),
            jax.ShapeDtypeStruct((8, n), F32),
        ],
        scratch_shapes=[pltpu.VMEM((tm + 2 * halo, D), F32)],
        input_output_aliases=aliases,
        compiler_params=_cparams(("arbitrary",)),
        name="a2",
    )(*args)


def _moe_kernel(tlo_ref, thi_ref, tile_ref, r0_ref, r1_ref, idx_hbm, h2_hbm, rw_ref, w1lo, w3lo, w2lo, w1hi, w3hi,
                w2hi, y_hbm,
                idx_smem, xin, xb_ref, wt_ref, ybuf, sem_idx, sem_g, sem_s, *, tb, n_tiles):
    i = pl.program_id(0)
    last = n_tiles - 1
    unroll = 8

    def idx_copy(t):
        slot = lax.rem(t, IDX_RING)
        return pltpu.make_async_copy(idx_hbm.at[tile_ref[t]], idx_smem.at[slot], sem_idx.at[slot])

    def tile_rows(j):
        return pl.ds(j * 8 if isinstance(j, int) else pl.multiple_of(j * 8, 8), 8)

    def gather_copy(row8, slot2, j):
        return pltpu.make_async_copy(h2_hbm.at[pl.ds(pl.multiple_of(row8, 8), 8), :],
                                     xin.at[slot2, tile_rows(j), :], sem_g.at[slot2])

    def gather_all(slot2):
        return pltpu.make_async_copy(h2_hbm.at[pl.ds(0, tb * 8), :], xin.at[slot2], sem_g.at[slot2])

    def scatter_copy(row8, slot2, j):
        return pltpu.make_async_copy(ybuf.at[slot2, tile_rows(j), :],
                                     y_hbm.at[pl.ds(pl.multiple_of(row8, 8), 8), :], sem_s.at[slot2])

    def rows(fn, count=tb):
        def body(jj, c):
            for u in range(unroll):
                fn(jj * unroll + u)
            return c
        full = count // unroll
        lax.fori_loop(0, full, body, 0)
        if not isinstance(count, int):
            def tail(j, c):
                fn(j)
                return c
            lax.fori_loop(full * unroll, count, tail, 0)

    def wait_scatter(t):
        s2 = lax.rem(t, 2)
        rows(lambda j: scatter_copy(0, s2, j).wait(), r1_ref[t] - r0_ref[t])

    s2 = lax.rem(i, 2)

    @pl.when(i == 0)
    def _():
        for t in range(3):
            idx_copy(t).start()
        for t in range(2):
            idx_copy(t).wait()
            rows(lambda j: gather_copy(idx_smem[t, 0, j], t, j).start())

    @pl.when(i + 2 <= last)
    def _():
        idx_copy(i + 2).wait()

    @pl.when(i + 3 <= last)
    def _():
        idx_copy(i + 3).start()

    gather_all(s2).wait()

    @pl.when(i >= 2)
    def _():
        wait_scatter(i - 2)

    for s in range(D // LANES):
        xb_ref[:, s * LANES:(s + 1) * LANES] = xin[s2, pl.ds(s, tb, stride=8), :].astype(BF16)
    wt_ref[...] = jnp.zeros(wt_ref.shape, F32)
    wt_ref[0:8, :] = rw_ref[0]
    wcol = wt_ref[...].T
    sn = lax.rem(jnp.minimum(i + 2, last), IDX_RING)
    tp = jnp.maximum(i - 1, 0)
    r0p = r0_ref[tp]
    r1p = jnp.where(i >= 1, r1_ref[tp], r0p)
    sp = lax.rem(tp, IDX_RING)
    for j in range(tb):
        gather_copy(idx_smem[sn, 0, j], s2, j).start()
    for j in range(tb):
        @pl.when(jnp.logical_and(j >= r0p, j < r1p))
        def _():
            scatter_copy(idx_smem[sp, 0, j], 1 - s2, j).start()
    xb = xb_ref[...]

    def ffn(w1, w3, w2):
        h1 = jnp.dot(xb, w1[0], preferred_element_type=F32)
        h3 = jnp.dot(xb, w3[0], preferred_element_type=F32)
        act = (h1 * _sigmoid(h1)) * h3
        return jnp.dot(act.astype(BF16), w2[0], preferred_element_type=F32)

    y = wcol[:, 0:1] * ffn(w1lo, w3lo, w2lo) + wcol[:, 1:2] * ffn(w1hi, w3hi, w2hi)
    _store_tiles(ybuf.at[s2], y)

    @pl.when(i == last)
    def _():
        wait_scatter(i - 1)
        gather_all(0).wait()
        gather_all(1).wait()


def _moe_plan(cls, rw, n_tok, tb):
    nt = n_tok // tb
    classes = jnp.arange(N_CLASSES, dtype=I32)
    _, order, wlo, whi = lax.sort((cls, jnp.arange(n_tok, dtype=I32), rw[0], rw[1]), num_keys=1, is_stable=True)
    counts = jnp.sum((cls[:, None] == classes[None, :]).astype(I32), axis=0)
    cend = jnp.cumsum(counts)
    bounds = jnp.sort(jnp.concatenate([jnp.arange(nt + 1, dtype=I32) * tb, cend]))
    p0, p1 = bounds[:-1], bounds[1:]
    tile = jnp.minimum(p0 // tb, nt - 1)
    scls = jnp.minimum(jnp.sum((p0[:, None] >= cend[None, :]).astype(I32), axis=1), N_CLASSES - 1)
    r0 = p0 - tile * tb
    r1 = p1 - tile * tb
    grp = scls // len(PAIR_LO)
    pair = scls % len(PAIR_LO)
    tlo = grp * EPG + jnp.asarray(PAIR_LO, I32)[pair]
    thi = grp * EPG + jnp.asarray(PAIR_HI, I32)[pair]
    idx = (order * 8).reshape(nt, 1, tb)
    rw_sorted = jnp.pad(jnp.stack([wlo.reshape(nt, tb), whi.reshape(nt, tb)], axis=1), ((0, 0), (0, 6), (0, 0)))
    return [v.astype(I32) for v in (tlo, thi, tile, r0, r1)], idx, rw_sorted, nt + N_CLASSES


def _moe_call(h2t, cls, rw, w1, w3, w2, *, tb=MOE_TB):
    n_tok = cls.shape[0]
    assert n_tok % tb == 0
    tables, idx, rw_sorted, n_steps = _moe_plan(cls, rw, n_tok, tb)
    wspec_lo = pl.BlockSpec((1, D, D), lambda i, lo, hi, tile, r0, r1: (lo[i], 0, 0))
    wspec_hi = pl.BlockSpec((1, D, D), lambda i, lo, hi, tile, r0, r1: (hi[i], 0, 0))
    grid_spec = pltpu.PrefetchScalarGridSpec(
        num_scalar_prefetch=5,
        grid=(n_steps,),
        in_specs=[pl.BlockSpec(memory_space=pl.ANY), pl.BlockSpec(memory_space=pl.ANY),
                  pl.BlockSpec((1, 8, tb), lambda i, lo, hi, tile, r0, r1: (tile[i], 0, 0)),
                  wspec_lo, wspec_lo, wspec_lo, wspec_hi, wspec_hi, wspec_hi],
        out_specs=pl.BlockSpec(memory_space=pl.ANY),
        scratch_shapes=[
            pltpu.SMEM((IDX_RING, 1, tb), I32),
            pltpu.VMEM((2, tb * 8, LANES), F32),
            pltpu.VMEM((tb, D), BF16),
            pltpu.VMEM((LANES, tb), F32),
            pltpu.VMEM((2, tb * 8, LANES), F32),
            pltpu.SemaphoreType.DMA((IDX_RING,)),
            pltpu.SemaphoreType.DMA((2,)),
            pltpu.SemaphoreType.DMA((2,)),
        ],
    )
    return pl.pallas_call(
        functools.partial(_moe_kernel, tb=tb, n_tiles=n_steps),
        grid_spec=grid_spec,
        out_shape=jax.ShapeDtypeStruct((n_tok * 8, LANES), F32),
        compiler_params=_cparams(("arbitrary",)),
        name="moe",
    )(*tables, idx, h2t, rw_sorted, w1, w3, w2, w1, w3, w2)


def _m1_kernel(xp_ref, xc_ref, xn_ref, yp_ref, yc_ref, yn_ref, modp_ref, modc_ref, g_ref, win_ref, wconv_ref,
               bconv_ref, bdq_ref, bdk_ref, bdv_ref, wgT_ref, bg_ref,
               x2_ref, q_ref, k_ref, v_ref, o_ref, gT_ref, xm_ref):
    tm = xc_ref.shape[0]
    i = pl.program_id(1)
    nt = pl.num_programs(1)
    xe = jnp.concatenate([xp_ref[...], xc_ref[...], xn_ref[...]], axis=0)
    ye = jnp.concatenate([_load_tiles(yp_ref, 8), _load_tiles(yc_ref, tm), _load_tiles(yn_ref, 8)], axis=0)
    xe = xe + modp_ref[0, 5:6, :] * ye
    x2_ref[...] = xe[8:8 + tm]
    h = _rms_mod(xe, g_ref[...], modc_ref[0, 0:1, :], modc_ref[0, 1:2, :]).astype(BF16)
    xm = jnp.dot(h, win_ref[:, 0:E], preferred_element_type=F32)
    z = jnp.dot(h, win_ref[:, E:2 * E], preferred_element_type=F32)
    o_ref[...] = _sigmoid(z[8:8 + tm]).astype(BF16)
    rid = lax.broadcasted_iota(I32, (tm + 16, 1), 0)
    inside = jnp.logical_and(jnp.logical_or(rid >= 8, i > 0), jnp.logical_or(rid < tm + 8, i < nt - 1))
    xm = jnp.where(inside, xm, 0.0)
    xm_ref[...] = xm
    acc = jnp.zeros((tm, E), F32) + bconv_ref[...]
    for kk in range(SHORT_K):
        acc = acc + wconv_ref[kk:kk + 1, :] * xm_ref[6 + kk:6 + kk + tm, :]
    xcb = (acc * _sigmoid(acc)).astype(BF16)
    xmb = xm[8:8 + tm].astype(BF16)
    nt_dims = (((1,), (1,)), ((), ()))
    gT = jnp.zeros((4 * NH, tm), F32) + bg_ref[...]
    kscale = DH ** -0.5
    for j in range(E // MXU_DIM):
        sl = slice(j * MXU_DIM, (j + 1) * MXU_DIM)
        qj = jnp.dot(xcb[:, sl], bdq_ref[j], preferred_element_type=F32)
        kj = jnp.dot(xcb[:, sl], bdk_ref[j], preferred_element_type=F32)
        vj = jnp.dot(xmb[:, sl], bdv_ref[j], preferred_element_type=F32)
        qb, kb, vb = qj.astype(BF16), kj.astype(BF16), vj.astype(BF16)
        gT = gT + lax.dot_general(wgT_ref[:, sl], qb, nt_dims, preferred_element_type=F32)
        gT = gT + lax.dot_general(wgT_ref[:, E + j * MXU_DIM:E + (j + 1) * MXU_DIM], kb, nt_dims,
                                  preferred_element_type=F32)
        gT = gT + lax.dot_general(wgT_ref[:, 2 * E + j * MXU_DIM:2 * E + (j + 1) * MXU_DIM], vb, nt_dims,
                                  preferred_element_type=F32)
        q_ref[:, sl] = qb
        k_ref[:, sl] = (kj * kscale).astype(BF16)
        v_ref[:, sl] = vb
    gT_ref[...] = gT


def _m1_call(xa, ya, modp, modc, g, win, wconv, bconv, bdq, bdk, bdv, wgT, bg, *, nb, seq, tm, y_row_off):
    n = nb * seq
    tps = seq // tm
    hb = tm // 8
    nb8 = n // 8
    yo = y_row_off // tm
    yo8 = y_row_off // 8
    blk = lambda b, i: b * tps + i
    const = lambda shape: pl.BlockSpec(shape, lambda b, i: (0,) * len(shape))
    prev8 = lambda b, i: jnp.maximum(blk(b, i) * hb - 1, 0)
    next8 = lambda b, i: jnp.minimum((blk(b, i) + 1) * hb, nb8 - 1)
    in_specs = [
        pl.BlockSpec((8, D), lambda b, i: (prev8(b, i), 0)),
        pl.BlockSpec((tm, D), lambda b, i: (blk(b, i), 0)),
        pl.BlockSpec((8, D), lambda b, i: (next8(b, i), 0)),
        pl.BlockSpec((8 * 8, LANES), lambda b, i: (yo8 + prev8(b, i), 0)),
        pl.BlockSpec((tm * 8, LANES), lambda b, i: (yo + blk(b, i), 0)),
        pl.BlockSpec((8 * 8, LANES), lambda b, i: (yo8 + next8(b, i), 0)),
        pl.BlockSpec((1, 8, D), lambda b, i: (b % modp.shape[0], 0, 0)),
        pl.BlockSpec((1, 8, D), lambda b, i: (b % modc.shape[0], 0, 0)),
        const((1, D)), const((D, 2 * E)), const((SHORT_K, E)), const((1, E)),
        const((E // MXU_DIM, MXU_DIM, MXU_DIM)), const((E // MXU_DIM, MXU_DIM, MXU_DIM)),
        const((E // MXU_DIM, MXU_DIM, MXU_DIM)), const((4 * NH, 3 * E)), const((4 * NH, 1)),
    ]
    tok = lambda w, dt: jax.ShapeDtypeStruct((n, w), dt)
    tspec = lambda w: pl.BlockSpec((tm, w), lambda b, i: (blk(b, i), 0))
    return pl.pallas_call(
        _m1_kernel,
        grid=(nb, tps),
        in_specs=in_specs,
        out_specs=[tspec(D), tspec(E), tspec(E), tspec(E), tspec(E),
                   pl.BlockSpec((4 * NH, tm), lambda b, i: (0, blk(b, i)))],
        out_shape=[tok(D, F32), tok(E, BF16), tok(E, BF16), tok(E, BF16), tok(E, BF16),
                   jax.ShapeDtypeStruct((4 * NH, n), F32)],
        scratch_shapes=[pltpu.VMEM((tm + 16, E), F32)],
        compiler_params=_cparams(("parallel", "arbitrary")),
        name="m1",
    )(xa, xa, xa, ya, ya, ya, modp, modc, g.reshape(1, D), win, wconv, bconv.reshape(1, E), bdq, bdk, bdv, wgT,
      bg.reshape(4 * NH, 1))


def _log_sigmoid(x):
    return jnp.minimum(x, 0.0) - jnp.log1p(jnp.exp(-jnp.abs(x)))


def _scan_masks(t, rev):
    r = lax.broadcasted_iota(I32, (t, t), 0)
    c = lax.broadcasted_iota(I32, (t, t), 1)
    tri = (c >= r) if rev else (c <= r)
    tri_t = (r >= c) if rev else (r <= c)
    return tri, tri_t, c == r


def _scan_step(q, k, v, i_row, f_row, c_ref, n_ref, m_ref, masks):
    t = q.shape[0]
    tri, tri_t, eye = masks
    lf = _log_sigmoid(f_row)
    lf_b = jnp.broadcast_to(lf, (t, t))
    b_col = jnp.sum(jnp.where(tri, lf_b, 0.0), axis=1, keepdims=True)
    lf_col = jnp.sum(jnp.where(eye, lf_b, 0.0), axis=1, keepdims=True)
    i_col = jnp.sum(jnp.where(eye, jnp.broadcast_to(i_row, (t, t)), 0.0), axis=1, keepdims=True)
    b_row = jnp.sum(jnp.where(tri_t, jnp.broadcast_to(lf_col, (t, t)), 0.0), axis=0, keepdims=True)
    m = m_ref[0:1, 0:1]
    dmat = jnp.where(tri, b_col - b_row + i_row, -jnp.inf)
    inter = b_col + m
    m_t = jnp.maximum(inter, jnp.max(dmat, axis=1, keepdims=True))
    wmat = jnp.exp(dmat - m_t)
    w_inter = jnp.exp(inter - m_t)
    s = lax.dot_general(q, k, (((1,), (1,)), ((), ())), preferred_element_type=F32) * wmat
    cb = c_ref[...].astype(BF16)
    num = jnp.dot(s.astype(BF16), v, preferred_element_type=F32) \
        + w_inter * jnp.dot(q, cb, preferred_element_type=F32)
    qn = jnp.sum(q.astype(F32) * n_ref[...], axis=1, keepdims=True)
    den = jnp.sum(s, axis=1, keepdims=True) + w_inter * qn
    h = num / jnp.maximum(jnp.abs(den), jnp.exp(-m_t))
    b_end = jnp.sum(lf, axis=1, keepdims=True)
    dlog = b_end - b_col + i_col
    m_new = jnp.maximum(b_end + m, jnp.max(dlog, axis=0, keepdims=True))
    wk = jnp.exp(dlog - m_new)
    decay = jnp.exp(b_end + m - m_new)
    kf = k.astype(F32) * wk
    c_ref[...] = decay * c_ref[...] + lax.dot_general(kf.astype(BF16), v, (((0,), (0,)), ((), ())),
                                                      preferred_element_type=F32)
    n_ref[...] = decay * n_ref[...] + jnp.sum(kf, axis=0, keepdims=True)
    m_ref[...] = jnp.broadcast_to(m_new, m_ref.shape)
    return h


def _scan_kernel(*refs, rev, final):
    qc, kc, vc, ic, fc, ql, kl, vl, il, fl = refs[:10]
    if final:
        hf_ref, o_ref, hng_ref, out_ref, c_ref, n_ref, m_ref = refs[10:]
    else:
        out_ref, c_ref, n_ref, m_ref = refs[10:]
    j = pl.program_id(2)
    masks = _scan_masks(CHUNK_T, rev)
    heads = [slice(hh * DH, (hh + 1) * DH) for hh in range(SCAN_HEADS)]

    def step(qr, kr, vr, ir, fr, hh):
        hs = heads[hh]
        return _scan_step(qr[0, :, hs], kr[0, :, hs], vr[0, :, hs], ir[hh], fr[hh],
                          c_ref.at[hh], n_ref.at[hh], m_ref.at[hh], masks)

    @pl.when(j == 0)
    def _():
        c_ref[...] = jnp.zeros(c_ref.shape, F32)
        n_ref[...] = jnp.zeros(n_ref.shape, F32)
        m_ref[...] = jnp.zeros(m_ref.shape, F32)
        for hh in range(SCAN_HEADS):
            step(qc, kc, vc, ic, fc, hh)

    @pl.when(j > 0)
    def _():
        for hh in range(SCAN_HEADS):
            h = step(ql, kl, vl, il, fl, hh)
            if final:
                hs = h + hf_ref[0, :, heads[hh]]
                mu = jnp.mean(hs, axis=-1, keepdims=True)
                hc = hs - mu
                var = jnp.mean(hc * hc, axis=-1, keepdims=True)
                hn = hc * lax.rsqrt(var + EPS) * hng_ref[:, heads[hh]]
                out_ref[0, :, heads[hh]] = (o_ref[0, :, heads[hh]] * hn.astype(BF16)).astype(BF16)
            else:
                out_ref[0, :, heads[hh]] = h


def _scan_call(qc, kc, vc, gc, ql, kl, vl, gl, hf, o, hng, *, nb, seq, rev):
    final = rev
    t = CHUNK_T
    ncl = seq // t
    if rev:
        cidx = lambda j: ncl - jnp.maximum(j, 1)
        irow, frow = 2 * NH, 3 * NH
    else:
        cidx = lambda j: jnp.maximum(j - 1, 0)
        irow, frow = 0, NH
    hp = SCAN_HEADS
    wid = hp * DH
    cspec = pl.BlockSpec((1, t, wid), lambda b, h, j: (b, 0, h))
    lspec = pl.BlockSpec((1, t, wid), lambda b, h, j: (b, cidx(j), h))
    gcs = lambda r0: pl.BlockSpec((hp, 1, t), lambda b, h, j: (r0 // hp + h, 0, b))
    gls = lambda r0: pl.BlockSpec((hp, 1, t), lambda b, h, j: (r0 // hp + h, 0, b * ncl + cidx(j)))
    in_specs = [cspec, cspec, cspec, gcs(irow), gcs(frow), lspec, lspec, lspec, gls(irow), gls(frow)]
    args = [qc, kc, vc, gc, gc, ql, kl, vl, gl, gl]
    if final:
        in_specs += [lspec, lspec, pl.BlockSpec((1, wid), lambda b, h, j: (0, h))]
        args += [hf, o, hng.reshape(1, E)]
    return pl.pallas_call(
        functools.partial(_scan_kernel, rev=rev, final=final),
        grid=(nb, NH // hp, ncl + 1),
        in_specs=in_specs,
        out_specs=lspec,
        out_shape=jax.ShapeDtypeStruct((nb, seq, E), BF16 if final else F32),
        scratch_shapes=[pltpu.VMEM((hp, DH, DH), F32), pltpu.VMEM((hp, 1, DH), F32),
                        pltpu.VMEM((hp, 8, LANES), F32)],
        compiler_params=_cparams(("parallel", "parallel", "arbitrary")),
        name="scan_bwd" if rev else "scan_fwd",
    )(*args)


def _m3_kernel(hg_ref, x_ref, mod_ref, wo_ref, gffn_ref, wrT_ref, br_ref, x1_ref, h2_ref, cls_ref, rw_ref):
    y = jnp.dot(hg_ref[...], wo_ref[...], preferred_element_type=F32)
    _tail(x_ref[...], y, mod_ref, gffn_ref, wrT_ref, br_ref, x1_ref, h2_ref, cls_ref, rw_ref)


def _m3_call(hg, x2d, mod, wo, gffn, wrT, br, *, seq, tm):
    n = x2d.shape[0]
    tps = seq // tm
    const = lambda shape: pl.BlockSpec(shape, lambda i: (0,) * len(shape))
    return pl.pallas_call(
        _m3_kernel,
        grid=(n // tm,),
        in_specs=[
            pl.BlockSpec((tm, E), lambda i: (i, 0)),
            pl.BlockSpec((tm, D), lambda i: (i, 0)),
            pl.BlockSpec((1, 8, D), lambda i: (i // tps, 0, 0)),
            const((E, D)), const((1, D)), const((N_EXPERTS, D)), const((N_EXPERTS, 1)),
        ],
        out_specs=[
            pl.BlockSpec((tm, D), lambda i: (i, 0)),
            pl.BlockSpec((tm * 8, LANES), lambda i: (i, 0)),
            pl.BlockSpec((8, tm), lambda i: (0, i)),
            pl.BlockSpec((8, tm), lambda i: (0, i)),
        ],
        out_shape=[
            jax.ShapeDtypeStruct((n, D), F32),
            jax.ShapeDtypeStruct((n * 8, LANES), F32),
            jax.ShapeDtypeStruct((8, n), I32),
            jax.ShapeDtypeStruct((8, n), F32),
        ],
        compiler_params=_cparams(("parallel",)),
        name="m3",
    )(hg, x2d, mod, wo, gffn.reshape(1, D), wrT, br.reshape(N_EXPERTS, 1))


def _final_kernel(x_ref, y_ref, mod_ref, g_ref, o_ref):
    x = x_ref[...] + mod_ref[0, 5:6, :] * _load_tiles(y_ref, x_ref.shape[0])
    ms = jnp.mean(x * x, axis=-1, keepdims=True)
    o_ref[...] = x * lax.rsqrt(ms + EPS) * g_ref[...]


def _final_call(x2d, y, mod, g, *, seq, tm):
    n = x2d.shape[0]
    tps = seq // tm
    return pl.pallas_call(
        _final_kernel,
        grid=(n // tm,),
        in_specs=[
            pl.BlockSpec((tm, D), lambda i: (i, 0)),
            pl.BlockSpec((tm * 8, LANES), lambda i: (i, 0)),
            pl.BlockSpec((1, 8, D), lambda i: (i // tps, 0, 0)),
            pl.BlockSpec((1, D), lambda i: (0, 0)),
        ],
        out_specs=pl.BlockSpec((tm, D), lambda i: (i, 0)),
        out_shape=jax.ShapeDtypeStruct((n, D), F32),
        compiler_params=_cparams(("parallel",)),
        name="final",
    )(x2d, y, mod, g.reshape(1, D))


def _blockdiag_tiles(w):
    per = MXU_DIM // 4
    w = w.reshape(E // MXU_DIM, per, 4, 4)
    eye = jnp.eye(per, dtype=w.dtype)
    t = w[:, :, :, None, :] * eye[None, :, None, :, None]
    return t.reshape(E // MXU_DIM, MXU_DIM, MXU_DIM)


def kernel(x, c, ctx, c_ctx, w_ada, b_ada, g_mix, g_ffn, g_final, w_in_ab, w_dw_a, b_dw_a, ln_g_a, ln_b_a, w_pool,
           b_pool, pool_scale, w_out_ab, w_in_c, w_conv_c, b_conv_c, w_q, w_k, w_v, w_gates, b_gates, hn_g, w_out_c,
           w_router, b_router, w1, w3, w2):
    nb, seq, _ = x.shape
    ctx_len = ctx.shape[1]
    assert ctx_len == CHUNK_T and seq % TM == 0 and seq % GRID_W == 0 and nb < 16
    n_lat = nb * seq
    n_ctx = nb * ctx_len
    x_lat = x.reshape(n_lat, D)
    x_ctx = ctx.reshape(n_ctx, D)

    s16 = jnp.zeros((16, D), F32).at[:nb].set(c).at[nb].set(c_ctx)
    mod = _mod_call(s16, w_ada, b_ada)
    pad8 = lambda m: jnp.pad(m.reshape(-1, 6, D), ((0, 0), (0, 2), (0, 0)))
    mod_lat = [pad8(mod[l, :nb]) for l in range(2)]
    mod_ctx = [pad8(mod[l, nb:nb + 1]) for l in range(2)]

    wrT = w_router.T
    bf = lambda w: w.astype(BF16)

    prm1 = (g_mix[0], bf(w_in_ab[0]), w_dw_a[0], b_dw_a[0], ln_g_a[0], ln_b_a[0])
    a_lat, bp_lat = _a1_call(x_lat, mod_lat[0], *prm1, seq=seq, tm=TM, seg=GRID_W)
    a_ctx, bp_ctx = _a1_call(x_ctx, mod_ctx[0], *prm1, seq=ctx_len, tm=ctx_len, seg=ctx_len)
    prm2 = (bf(w_pool[0]), b_pool[0], pool_scale[0], bf(w_out_ab[0]), g_ffn[0], wrT, b_router)
    n_tot = n_lat + n_ctx
    x1_lat, h2t, cls_lat, rw_lat = _a2_call(bp_lat, a_lat, x_lat, mod_lat[0], *prm2, None, seq=seq, tm=TM,
                                            stride=GRID_W, npos=seq // GRID_W, use_halo=True, h2_rows=n_tot,
                                            h2_row_off=0)
    x1_ctx, h2t, cls_ctx, rw_ctx = _a2_call(bp_ctx, a_ctx, x_ctx, mod_ctx[0], *prm2, h2t, seq=ctx_len, tm=ctx_len,
                                            stride=1, npos=ctx_len, use_halo=False, h2_rows=n_tot,
                                            h2_row_off=n_lat)
    cls0 = jnp.concatenate([cls_lat[0], cls_ctx[0]])
    rw0 = jnp.concatenate([rw_lat, rw_ctx], axis=1)
    y0 = _moe_call(h2t, cls0, rw0, bf(w1[0]), bf(w3[0]), bf(w2[0]))

    wgT = bf(w_gates[0].T)
    prm3 = (g_mix[1], bf(w_in_c[0]), w_conv_c[0], b_conv_c[0], bf(_blockdiag_tiles(w_q[0])),
            bf(_blockdiag_tiles(w_k[0])), bf(_blockdiag_tiles(w_v[0])), wgT, b_gates[0])
    x2_lat, ql, kl, vl, ol, gl = _m1_call(x1_lat, y0, mod_lat[0], mod_lat[1], *prm3, nb=nb, seq=seq, tm=TM1,
                                          y_row_off=0)
    _, qc, kc, vc, _, gc = _m1_call(x1_ctx, y0, mod_ctx[0], mod_ctx[1], *prm3, nb=nb, seq=ctx_len, tm=ctx_len,
                                    y_row_off=n_lat)
    r3 = lambda t, length: t.reshape(nb, length, E)
    ql, kl, vl, ol = (r3(t, seq) for t in (ql, kl, vl, ol))
    qc, kc, vc = (r3(t, ctx_len) for t in (qc, kc, vc))
    gl = gl.reshape(4 * NH, 1, n_lat)
    gc = gc.reshape(4 * NH, 1, n_ctx)
    hf = _scan_call(qc, kc, vc, gc, ql, kl, vl, gl, None, None, None, nb=nb, seq=seq, rev=False)
    hg = _scan_call(qc, kc, vc, gc, ql, kl, vl, gl, hf, ol, hn_g[0], nb=nb, seq=seq, rev=True)
    x3_lat, h2t1, cls1, rw1 = _m3_call(hg.reshape(n_lat, E), x2_lat, mod_lat[1], bf(w_out_c[0]), g_ffn[1], wrT,
                                       b_router, seq=seq, tm=TM)
    y1 = _moe_call(h2t1, cls1[0], rw1, bf(w1[1]), bf(w3[1]), bf(w2[1]))
    out = _final_call(x3_lat, y1, mod_lat[1], g_final, seq=seq, tm=TM)
    return out.reshape(nb, seq, D)
```

```python
import functools

import jax
import jax.numpy as jnp
from jax import lax
from jax.experimental import pallas as pl
from jax.experimental.pallas import tpu as pltpu

F32 = jnp.float32
BF16 = jnp.bfloat16
I32 = jnp.int32
HIGHEST = lax.Precision.HIGHEST

D = 1024
EPS = 1e-6
GRID_W = 64
CONV_K = 31
POOL_WINDOWS = (2, 4, 8, 16)
POOL_GC = 256
E = 2 * D
NH = 4
DH = E // NH
SHORT_K = 4
N_EXPERTS = 16
EPG = 4
N_CLASSES = 24
PAIR_LO = (0, 0, 0, 1, 1, 2)
PAIR_HI = (1, 2, 3, 2, 3, 3)

LANES = 128
MXU_DIM = 256
TM = 512
TM1 = 256
CHUNK_T = 256
SCAN_HEADS = 4
MOE_TB = 256
IDX_RING = 8
VMEM_LIMIT = 56 * 1024 * 1024


def _cparams(sem):
    return pltpu.CompilerParams(dimension_semantics=sem, vmem_limit_bytes=VMEM_LIMIT)


def _sigmoid(x):
    return jax.nn.sigmoid(x)


def _rms_mod(x, g, shift, scale):
    ms = jnp.mean(x * x, axis=-1, keepdims=True)
    return (x * lax.rsqrt(ms + EPS) * g) * (1.0 + scale) + shift


def _mod_kernel(s_ref, w_ref, b_ref, o_ref):
    s = s_ref[...]
    s = s * _sigmoid(s)
    o_ref[0] = jnp.dot(s, w_ref[0], preferred_element_type=F32, precision=HIGHEST) + b_ref[0]


def _mod_call(s16, w_ada, b_ada):
    depth = w_ada.shape[0]
    tn = 1536
    return pl.pallas_call(
        _mod_kernel,
        grid=(depth, 6 * D // tn),
        in_specs=[
            pl.BlockSpec((16, D), lambda l, n: (0, 0)),
            pl.BlockSpec((1, D, tn), lambda l, n: (l, 0, n)),
            pl.BlockSpec((1, 1, tn), lambda l, n: (l, 0, n)),
        ],
        out_specs=pl.BlockSpec((1, 16, tn), lambda l, n: (l, 0, n)),
        out_shape=jax.ShapeDtypeStruct((depth, 16, 6 * D), F32),
        compiler_params=_cparams(("parallel", "parallel")),
        name="mod",
    )(s16, w_ada, b_ada.reshape(depth, 1, 6 * D))


def _a1_kernel(x_ref, mod_ref, g_ref, win_ref, wdw_ref, bdw_ref, lng_ref, lnb_ref,
               a_ref, bp_ref, pad_ref, conv_ref, shf_ref, *, seg):
    tm = x_ref.shape[0]
    nseg = tm // seg
    h = _rms_mod(x_ref[...], g_ref[...], mod_ref[0, 0:1, :], mod_ref[0, 1:2, :])
    u = jnp.dot(h.astype(BF16), win_ref[...], preferred_element_type=F32)
    a = u[:, :D] * _sigmoid(u[:, D:2 * D])
    bp_ref[...] = u[:, 2 * D:]
    zeros16 = jnp.zeros((nseg, 16, D), F32)
    pad_ref[:, 0:16, :] = zeros16
    pad_ref[:, seg + 16:seg + 32, :] = zeros16
    pad_ref[:, 16:16 + seg, :] = a.reshape(nseg, seg, D)

    def seg_body(s, carry):
        for cs in range(D // LANES):
            sl = slice(cs * LANES, (cs + 1) * LANES)
            acc = jnp.zeros((seg, LANES), F32)
            for r in range(8):
                starts = [st for st in range(r, CONV_K + 1, 8) if st >= 1]
                lo = starts[0]
                nrow = starts[-1] - lo + seg
                if r:
                    shf_ref[r, 0:nrow, :] = pad_ref[s, pl.ds(lo, nrow), sl]
                    src = lambda st: shf_ref[r, st - lo:st - lo + seg, :]
                else:
                    src = lambda st: pad_ref[s, pl.ds(st, seg), sl]
                for st in starts:
                    acc = acc + wdw_ref[st - 1:st, sl] * src(st)
            conv_ref[pl.ds(pl.multiple_of(s * seg, seg), seg), sl] = acc + bdw_ref[:, sl]
        return carry

    lax.fori_loop(0, nseg, seg_body, 0)
    cv = conv_ref[...]
    mu = jnp.mean(cv, axis=-1, keepdims=True)
    cc = cv - mu
    var = jnp.mean(cc * cc, axis=-1, keepdims=True)
    y = cc * lax.rsqrt(var + EPS) * lng_ref[...] + lnb_ref[...]
    a_ref[...] = (y * _sigmoid(y)).astype(BF16)


def _a1_call(x2d, mod, g, win, wdw, bdw, lng, lnb, *, seq, tm, seg):
    n = x2d.shape[0]
    tps = seq // tm
    row = lambda v: v.reshape(1, -1)
    const = lambda shape: pl.BlockSpec(shape, lambda i: (0,) * len(shape))
    return pl.pallas_call(
        functools.partial(_a1_kernel, seg=seg),
        grid=(n // tm,),
        in_specs=[
            pl.BlockSpec((tm, D), lambda i: (i, 0)),
            pl.BlockSpec((1, 8, D), lambda i: ((i // tps) % mod.shape[0], 0, 0)),
            const((1, D)), const((D, 3 * D)), const((CONV_K, D)), const((1, D)), const((1, D)), const((1, D)),
        ],
        out_specs=[pl.BlockSpec((tm, D), lambda i: (i, 0)), pl.BlockSpec((tm, D), lambda i: (i, 0))],
        out_shape=[jax.ShapeDtypeStruct((n, D), BF16), jax.ShapeDtypeStruct((n, D), F32)],
        scratch_shapes=[pltpu.VMEM((tm // seg, seg + 32, D), F32), pltpu.VMEM((tm, D), F32),
                        pltpu.VMEM((8, seg + 24, LANES), F32)],
        compiler_params=_cparams(("parallel",)),
        name="a1",
    )(x2d, mod, row(g), win, wdw, row(bdw), row(lng), row(lnb))


def _store_tiles(ref, val):
    n = val.shape[0]
    for s in range(D // LANES):
        ref[pl.ds(s, n, stride=8), :] = val[:, s * LANES:(s + 1) * LANES]


def _load_tiles(ref, n):
    return jnp.concatenate([ref[pl.ds(s, n, stride=8), :] for s in range(D // LANES)], axis=-1)


def _tail(x, y, mod_ref, gffn_ref, wrT_ref, br_ref, x1_ref, h2_ref, cls_ref, rw_ref):
    tm = x.shape[0]
    x1 = x + mod_ref[0, 2:3, :] * y
    x1_ref[...] = x1
    h2 = _rms_mod(x1, gffn_ref[...], mod_ref[0, 3:4, :], mod_ref[0, 4:5, :])
    logits = lax.dot_general(wrT_ref[...], h2, (((1,), (1,)), ((), ())),
                             precision=HIGHEST, preferred_element_type=F32)
    sc = _sigmoid(logits)
    bi = sc + br_ref[...]
    gs = []
    for g in range(N_EXPERTS // EPG):
        v = [bi[EPG * g + j:EPG * g + j + 1, :] for j in range(EPG)]
        best = v[0] + v[1]
        for (p, q) in ((0, 2), (0, 3), (1, 2), (1, 3), (2, 3)):
            best = jnp.maximum(best, v[p] + v[q])
        gs.append(best)
    gsel = jnp.zeros((1, tm), I32)
    gbest = gs[0]
    for g in range(1, N_EXPERTS // EPG):
        better = gs[g] > gbest
        gsel = jnp.where(better, g, gsel)
        gbest = jnp.where(better, gs[g], gbest)

    def pick(arr, j):
        out = arr[j:j + 1, :]
        for g in range(1, N_EXPERTS // EPG):
            out = jnp.where(gsel == g, arr[EPG * g + j:EPG * g + j + 1, :], out)
        return out

    cand = [pick(bi, j) for j in range(EPG)]
    scs = [pick(sc, j) for j in range(EPG)]
    i1 = jnp.zeros((1, tm), I32)
    b1 = cand[0]
    for j in range(1, EPG):
        better = cand[j] > b1
        i1 = jnp.where(better, j, i1)
        b1 = jnp.where(better, cand[j], b1)
    neg = jnp.full((1, tm), -jnp.inf, F32)
    i2 = jnp.zeros((1, tm), I32)
    b2 = neg
    for j in range(EPG):
        cj = jnp.where(i1 == j, neg, cand[j])
        better = cj > b2
        i2 = jnp.where(better, j, i2)
        b2 = jnp.where(better, cj, b2)
    w1 = jnp.zeros((1, tm), F32)
    w2 = jnp.zeros((1, tm), F32)
    for j in range(EPG):
        w1 = jnp.where(i1 == j, scs[j], w1)
        w2 = jnp.where(i2 == j, scs[j], w2)
    den = w1 + w2
    w1 = w1 / den
    w2 = w2 / den
    first_lo = i1 < i2
    lo = jnp.minimum(i1, i2)
    hi = jnp.maximum(i1, i2)
    w_lo = jnp.where(first_lo, w1, w2)
    w_hi = jnp.where(first_lo, w2, w1)
    pidx = jnp.where(lo == 0, hi - 1, jnp.where(lo == 1, hi + 1, 5))
    cls = gsel * len(PAIR_LO) + pidx
    cls_ref[...] = jnp.broadcast_to(cls, (8, tm))
    rw_ref[...] = jnp.concatenate([w_lo, w_hi, jnp.zeros((6, tm), F32)], axis=0)
    _store_tiles(h2_ref, h2)


def _a2_kernel(bpp_ref, bpc_ref, bpn_ref, a_ref, x_ref, mod_ref, wpool_ref, bpool_ref, pscale_ref, wo_ref,
               gffn_ref, wrT_ref, br_ref, x1_ref, h2_ref, cls_ref, rw_ref, pb_ref, *, stride, npos, use_halo,
               ntile, nt):
    tm = x_ref.shape[0]
    halo = 8 * stride
    i = lax.rem(jnp.minimum(pl.program_id(0), ntile - 1), nt)
    cur = bpc_ref[...]
    if use_halo:
        pb_ref[0:halo, :] = jnp.where(i > 0, bpp_ref[...], 0.0)
        pb_ref[halo + tm:halo + tm + halo, :] = jnp.where(i < nt - 1, bpn_ref[...], 0.0)
    else:
        pb_ref[0:halo, :] = jnp.zeros((halo, D), F32)
        pb_ref[halo + tm:halo + tm + halo, :] = jnp.zeros((halo, D), F32)
    pb_ref[halo:halo + tm, :] = cur
    pos = (i * tm + lax.broadcasted_iota(I32, (tm, 1), 0)) // stride
    y = jnp.dot(a_ref[...], wo_ref[0:D, :], preferred_element_type=F32)
    for g, w in enumerate(POOL_WINDOWS):
        gsl = slice(g * POOL_GC, (g + 1) * POOL_GC)
        left = w // 2
        right = w - 1 - left
        tot = jnp.zeros((tm, POOL_GC), F32)
        for j in range(-left, right + 1):
            tot = tot + pb_ref[halo + j * stride:halo + j * stride + tm, gsl]
        cnt = (jnp.minimum(pos + right + 1, npos) - jnp.maximum(pos - left, 0)).astype(F32)
        pooled = tot / cnt - cur[:, gsl]
        pg = jnp.dot(pooled.astype(BF16), wpool_ref[g], preferred_element_type=F32) + bpool_ref[g:g + 1, :]
        pg = pg * pscale_ref[:, gsl]
        y = y + jnp.dot(pg.astype(BF16), wo_ref[D + g * POOL_GC:D + (g + 1) * POOL_GC, :],
                        preferred_element_type=F32)
    _tail(x_ref[...], y, mod_ref, gffn_ref, wrT_ref, br_ref, x1_ref, h2_ref, cls_ref, rw_ref)


def _a2_call(bp, a, x2d, mod, wpool, bpool, pscale, wo, gffn, wrT, br, h2_prev, *, seq, tm, stride, npos, use_halo,
             h2_rows, h2_row_off):
    n = x2d.shape[0]
    tps = seq // tm
    ntile = n // tm
    halo = 8 * stride
    off = h2_row_off // tm
    steps = ntile if h2_prev is not None else h2_rows // tm
    blk = lambda i: jnp.minimum(i, ntile - 1)
    row = lambda v: v.reshape(1, -1)
    const = lambda shape: pl.BlockSpec(shape, lambda i: (0,) * len(shape))
    in_specs = [
        pl.BlockSpec((tm, D), lambda i: (jnp.maximum(blk(i) - 1, 0), 0)),
        pl.BlockSpec((tm, D), lambda i: (blk(i), 0)),
        pl.BlockSpec((tm, D), lambda i: (jnp.minimum(blk(i) + 1, ntile - 1), 0)),
        pl.BlockSpec((tm, D), lambda i: (blk(i), 0)),
        pl.BlockSpec((tm, D), lambda i: (blk(i), 0)),
        pl.BlockSpec((1, 8, D), lambda i: ((blk(i) // tps) % mod.shape[0], 0, 0)),
        const((4, POOL_GC, POOL_GC)), const((4, POOL_GC)), const((1, D)), const((2 * D, D)),
        const((1, D)), const((N_EXPERTS, D)), const((N_EXPERTS, 1)),
    ]
    args = [bp, bp, bp, a, x2d, mod, wpool, bpool, row(pscale), wo, row(gffn), wrT, br.reshape(N_EXPERTS, 1)]
    aliases = {}
    body = functools.partial(_a2_kernel, stride=stride, npos=npos, use_halo=use_halo, ntile=ntile, nt=tps)
    if h2_prev is not None:
        in_specs.append(pl.BlockSpec(memory_space=pl.ANY))
        args.append(h2_prev)
        aliases = {len(args) - 1: 1}
        inner = body
        body = lambda *refs: inner(*refs[:13], *refs[14:])
    return pl.pallas_call(
        body,
        grid=(steps,),
        in_specs=in_specs,
        out_specs=[
            pl.BlockSpec((tm, D), lambda i: (blk(i), 0)),
            pl.BlockSpec((tm * 8, LANES), lambda i: (off + i, 0)),
            pl.BlockSpec((8, tm), lambda i: (0, blk(i))),
            pl.BlockSpec((8, tm), lambda i: (0, blk(i))),
        ],
        out_shape=[
            jax.ShapeDtypeStruct((n, D), F32),
            jax.ShapeDtypeStruct((h2_rows * 8, LANES), F32),
            jax.ShapeDtypeStruct((8, n), I32),
            jax.ShapeDtypeStruct((8, n), F32),
        ],
        scratch_shapes=[pltpu.VMEM((tm + 2 * halo, D), F32)],
        input_output_aliases=aliases,
        compiler_params=_cparams(("arbitrary",)),
        name="a2",
    )(*args)


def _moe_kernel(tlo_ref, thi_ref, tile_ref, r0_ref, r1_ref, idx_hbm, h2_hbm, rw_ref, w1lo, w3lo, w2lo, w1hi, w3hi,
                w2hi, y_hbm,
                idx_smem, xin, xb_ref, wt_ref, ybuf, sem_idx, sem_g, sem_s, *, tb, n_tiles):
    i = pl.program_id(0)
    last = n_tiles - 1
    unroll = 8

    def idx_copy(t):
        slot = lax.rem(t, IDX_RING)
        return pltpu.make_async_copy(idx_hbm.at[tile_ref[t]], idx_smem.at[slot], sem_idx.at[slot])

    def tile_rows(j):
        return pl.ds(j * 8 if isinstance(j, int) else pl.multiple_of(j * 8, 8), 8)

    def gather_copy(row8, slot2, j):
        return pltpu.make_async_copy(h2_hbm.at[pl.ds(pl.multiple_of(row8, 8), 8), :],
                                     xin.at[slot2, tile_rows(j), :], sem_g.at[slot2])

    def gather_all(slot2):
        return pltpu.make_async_copy(h2_hbm.at[pl.ds(0, tb * 8), :], xin.at[slot2], sem_g.at[slot2])

    def scatter_copy(row8, slot2, j):
        return pltpu.make_async_copy(ybuf.at[slot2, tile_rows(j), :],
                                     y_hbm.at[pl.ds(pl.multiple_of(row8, 8), 8), :], sem_s.at[slot2])

    def rows(fn, count=tb):
        def body(jj, c):
            for u in range(unroll):
                fn(jj * unroll + u)
            return c
        full = count // unroll
        lax.fori_loop(0, full, body, 0)
        if not isinstance(count, int):
            def tail(j, c):
                fn(j)
                return c
            lax.fori_loop(full * unroll, count, tail, 0)

    def wait_scatter(t):
        s2 = lax.rem(t, 2)
        rows(lambda j: scatter_copy(0, s2, j).wait(), r1_ref[t] - r0_ref[t])

    s2 = lax.rem(i, 2)

    @pl.when(i == 0)
    def _():
        for t in range(3):
            idx_copy(t).start()
        for t in range(2):
            idx_copy(t).wait()
            rows(lambda j: gather_copy(idx_smem[t, 0, j], t, j).start())

    @pl.when(i + 2 <= last)
    def _():
        idx_copy(i + 2).wait()

    @pl.when(i + 3 <= last)
    def _():
        idx_copy(i + 3).start()

    gather_all(s2).wait()

    @pl.when(i >= 2)
    def _():
        wait_scatter(i - 2)

    for s in range(D // LANES):
        xb_ref[:, s * LANES:(s + 1) * LANES] = xin[s2, pl.ds(s, tb, stride=8), :].astype(BF16)
    wt_ref[...] = jnp.zeros(wt_ref.shape, F32)
    wt_ref[0:8, :] = rw_ref[0]
    wcol = wt_ref[...].T
    sn = lax.rem(jnp.minimum(i + 2, last), IDX_RING)
    tp = jnp.maximum(i - 1, 0)
    r0p = r0_ref[tp]
    r1p = jnp.where(i >= 1, r1_ref[tp], r0p)
    sp = lax.rem(tp, IDX_RING)
    for j in range(tb):
        gather_copy(idx_smem[sn, 0, j], s2, j).start()
    for j in range(tb):
        @pl.when(jnp.logical_and(j >= r0p, j < r1p))
        def _():
            scatter_copy(idx_smem[sp, 0, j], 1 - s2, j).start()
    xb = xb_ref[...]

    def ffn(w1, w3, w2):
        h1 = jnp.dot(xb, w1[0], preferred_element_type=F32)
        h3 = jnp.dot(xb, w3[0], preferred_element_type=F32)
        act = (h1 * _sigmoid(h1)) * h3
        return jnp.dot(act.astype(BF16), w2[0], preferred_element_type=F32)

    y = wcol[:, 0:1] * ffn(w1lo, w3lo, w2lo) + wcol[:, 1:2] * ffn(w1hi, w3hi, w2hi)
    _store_tiles(ybuf.at[s2], y)

    @pl.when(i == last)
    def _():
        wait_scatter(i - 1)
        gather_all(0).wait()
        gather_all(1).wait()


def _moe_plan(cls, rw, n_tok, tb):
    nt = n_tok // tb
    classes = jnp.arange(N_CLASSES, dtype=I32)
    _, order, wlo, whi = lax.sort((cls, jnp.arange(n_tok, dtype=I32), rw[0], rw[1]), num_keys=1, is_stable=True)
    counts = jnp.sum((cls[:, None] == classes[None, :]).astype(I32), axis=0)
    cend = jnp.cumsum(counts)
    bounds = jnp.sort(jnp.concatenate([jnp.arange(nt + 1, dtype=I32) * tb, cend]))
    p0, p1 = bounds[:-1], bounds[1:]
    tile = jnp.minimum(p0 // tb, nt - 1)
    scls = jnp.minimum(jnp.sum((p0[:, None] >= cend[None, :]).astype(I32), axis=1), N_CLASSES - 1)
    r0 = p0 - tile * tb
    r1 = p1 - tile * tb
    grp = scls // len(PAIR_LO)
    pair = scls % len(PAIR_LO)
    tlo = grp * EPG + jnp.asarray(PAIR_LO, I32)[pair]
    thi = grp * EPG + jnp.asarray(PAIR_HI, I32)[pair]
    idx = (order * 8).reshape(nt, 1, tb)
    rw_sorted = jnp.pad(jnp.stack([wlo.reshape(nt, tb), whi.reshape(nt, tb)], axis=1), ((0, 0), (0, 6), (0, 0)))
    return [v.astype(I32) for v in (tlo, thi, tile, r0, r1)], idx, rw_sorted, nt + N_CLASSES


def _moe_call(h2t, cls, rw, w1, w3, w2, layer, *, tb=MOE_TB):
    n_tok = cls.shape[0]
    assert n_tok % tb == 0
    tables, idx, rw_sorted, n_steps = _moe_plan(cls, rw, n_tok, tb)
    tables[0] = tables[0] + layer * N_EXPERTS
    tables[1] = tables[1] + layer * N_EXPERTS
    wspec_lo = pl.BlockSpec((1, D, D), lambda i, lo, hi, tile, r0, r1: (lo[i], 0, 0))
    wspec_hi = pl.BlockSpec((1, D, D), lambda i, lo, hi, tile, r0, r1: (hi[i], 0, 0))
    grid_spec = pltpu.PrefetchScalarGridSpec(
        num_scalar_prefetch=5,
        grid=(n_steps,),
        in_specs=[pl.BlockSpec(memory_space=pl.ANY), pl.BlockSpec(memory_space=pl.ANY),
                  pl.BlockSpec((1, 8, tb), lambda i, lo, hi, tile, r0, r1: (tile[i], 0, 0)),
                  wspec_lo, wspec_lo, wspec_lo, wspec_hi, wspec_hi, wspec_hi],
        out_specs=pl.BlockSpec(memory_space=pl.ANY),
        scratch_shapes=[
            pltpu.SMEM((IDX_RING, 1, tb), I32),
            pltpu.VMEM((2, tb * 8, LANES), F32),
            pltpu.VMEM((tb, D), BF16),
            pltpu.VMEM((LANES, tb), F32),
            pltpu.VMEM((2, tb * 8, LANES), F32),
            pltpu.SemaphoreType.DMA((IDX_RING,)),
            pltpu.SemaphoreType.DMA((2,)),
            pltpu.SemaphoreType.DMA((2,)),
        ],
    )
    return pl.pallas_call(
        functools.partial(_moe_kernel, tb=tb, n_tiles=n_steps),
        grid_spec=grid_spec,
        out_shape=jax.ShapeDtypeStruct((n_tok * 8, LANES), F32),
        compiler_params=_cparams(("arbitrary",)),
        name="moe",
    )(*tables, idx, h2t, rw_sorted, w1, w3, w2, w1, w3, w2)


def _m1_kernel(xp_ref, xc_ref, xn_ref, yp_ref, yc_ref, yn_ref, modp_ref, modc_ref, g_ref, win_ref, wconv_ref,
               bconv_ref, bdq_ref, bdk_ref, bdv_ref, wgT_ref, bg_ref,
               x2_ref, q_ref, k_ref, v_ref, o_ref, gT_ref, xm_ref):
    tm = xc_ref.shape[0]
    i = pl.program_id(1)
    nt = pl.num_programs(1)
    xe = jnp.concatenate([xp_ref[...], xc_ref[...], xn_ref[...]], axis=0)
    ye = jnp.concatenate([_load_tiles(yp_ref, 8), _load_tiles(yc_ref, tm), _load_tiles(yn_ref, 8)], axis=0)
    xe = xe + modp_ref[0, 5:6, :] * ye
    x2_ref[...] = xe[8:8 + tm]
    h = _rms_mod(xe, g_ref[...], modc_ref[0, 0:1, :], modc_ref[0, 1:2, :]).astype(BF16)
    xm = jnp.dot(h, win_ref[:, 0:E], preferred_element_type=F32)
    z = jnp.dot(h, win_ref[:, E:2 * E], preferred_element_type=F32)
    o_ref[...] = _sigmoid(z[8:8 + tm]).astype(BF16)
    rid = lax.broadcasted_iota(I32, (tm + 16, 1), 0)
    inside = jnp.logical_and(jnp.logical_or(rid >= 8, i > 0), jnp.logical_or(rid < tm + 8, i < nt - 1))
    xm = jnp.where(inside, xm, 0.0)
    xm_ref[...] = xm
    acc = jnp.zeros((tm, E), F32) + bconv_ref[...]
    for kk in range(SHORT_K):
        acc = acc + wconv_ref[kk:kk + 1, :] * xm_ref[6 + kk:6 + kk + tm, :]
    xcb = (acc * _sigmoid(acc)).astype(BF16)
    xmb = xm[8:8 + tm].astype(BF16)
    nt_dims = (((1,), (1,)), ((), ()))
    gT = jnp.zeros((4 * NH, tm), F32) + bg_ref[...]
    kscale = DH ** -0.5
    for j in range(E // MXU_DIM):
        sl = slice(j * MXU_DIM, (j + 1) * MXU_DIM)
        qj = jnp.dot(xcb[:, sl], bdq_ref[j], preferred_element_type=F32)
        kj = jnp.dot(xcb[:, sl], bdk_ref[j], preferred_element_type=F32)
        vj = jnp.dot(xmb[:, sl], bdv_ref[j], preferred_element_type=F32)
        qb, kb, vb = qj.astype(BF16), kj.astype(BF16), vj.astype(BF16)
        gT = gT + lax.dot_general(wgT_ref[:, sl], qb, nt_dims, preferred_element_type=F32)
        gT = gT + lax.dot_general(wgT_ref[:, E + j * MXU_DIM:E + (j + 1) * MXU_DIM], kb, nt_dims,
                                  preferred_element_type=F32)
        gT = gT + lax.dot_general(wgT_ref[:, 2 * E + j * MXU_DIM:2 * E + (j + 1) * MXU_DIM], vb, nt_dims,
                                  preferred_element_type=F32)
        q_ref[:, sl] = qb
        k_ref[:, sl] = (kj * kscale).astype(BF16)
        v_ref[:, sl] = vb
    gT_ref[...] = gT


def _m1_call(xa, ya, modp, modc, g, win, wconv, bconv, bdq, bdk, bdv, wgT, bg, *, nb, seq, tm, y_row_off):
    n = nb * seq
    tps = seq // tm
    hb = tm // 8
    nb8 = n // 8
    yo = y_row_off // tm
    yo8 = y_row_off // 8
    blk = lambda b, i: b * tps + i
    const = lambda shape: pl.BlockSpec(shape, lambda b, i: (0,) * len(shape))
    prev8 = lambda b, i: jnp.maximum(blk(b, i) * hb - 1, 0)
    next8 = lambda b, i: jnp.minimum((blk(b, i) + 1) * hb, nb8 - 1)
    in_specs = [
        pl.BlockSpec((8, D), lambda b, i: (prev8(b, i), 0)),
        pl.BlockSpec((tm, D), lambda b, i: (blk(b, i), 0)),
        pl.BlockSpec((8, D), lambda b, i: (next8(b, i), 0)),
        pl.BlockSpec((8 * 8, LANES), lambda b, i: (yo8 + prev8(b, i), 0)),
        pl.BlockSpec((tm * 8, LANES), lambda b, i: (yo + blk(b, i), 0)),
        pl.BlockSpec((8 * 8, LANES), lambda b, i: (yo8 + next8(b, i), 0)),
        pl.BlockSpec((1, 8, D), lambda b, i: (b % modp.shape[0], 0, 0)),
        pl.BlockSpec((1, 8, D), lambda b, i: (b % modc.shape[0], 0, 0)),
        const((1, D)), const((D, 2 * E)), const((SHORT_K, E)), const((1, E)),
        const((E // MXU_DIM, MXU_DIM, MXU_DIM)), const((E // MXU_DIM, MXU_DIM, MXU_DIM)),
        const((E // MXU_DIM, MXU_DIM, MXU_DIM)), const((4 * NH, 3 * E)), const((4 * NH, 1)),
    ]
    tok = lambda w, dt: jax.ShapeDtypeStruct((n, w), dt)
    tspec = lambda w: pl.BlockSpec((tm, w), lambda b, i: (blk(b, i), 0))
    return pl.pallas_call(
        _m1_kernel,
        grid=(nb, tps),
        in_specs=in_specs,
        out_specs=[tspec(D), tspec(E), tspec(E), tspec(E), tspec(E),
                   pl.BlockSpec((4 * NH, tm), lambda b, i: (0, blk(b, i)))],
        out_shape=[tok(D, F32), tok(E, BF16), tok(E, BF16), tok(E, BF16), tok(E, BF16),
                   jax.ShapeDtypeStruct((4 * NH, n), F32)],
        scratch_shapes=[pltpu.VMEM((tm + 16, E), F32)],
        compiler_params=_cparams(("parallel", "arbitrary")),
        name="m1",
    )(xa, xa, xa, ya, ya, ya, modp, modc, g.reshape(1, D), win, wconv, bconv.reshape(1, E), bdq, bdk, bdv, wgT,
      bg.reshape(4 * NH, 1))


def _log_sigmoid(x):
    return jnp.minimum(x, 0.0) - jnp.log1p(jnp.exp(-jnp.abs(x)))


def _scan_masks(t, rev):
    r = lax.broadcasted_iota(I32, (t, t), 0)
    c = lax.broadcasted_iota(I32, (t, t), 1)
    tri = (c >= r) if rev else (c <= r)
    tri_t = (r >= c) if rev else (r <= c)
    return tri, tri_t, c == r


def _scan_step(q, k, v, i_row, f_row, c_ref, n_ref, m_ref, masks):
    t = q.shape[0]
    tri, tri_t, eye = masks
    lf = _log_sigmoid(f_row)
    lf_b = jnp.broadcast_to(lf, (t, t))
    b_col = jnp.sum(jnp.where(tri, lf_b, 0.0), axis=1, keepdims=True)
    lf_col = jnp.sum(jnp.where(eye, lf_b, 0.0), axis=1, keepdims=True)
    i_col = jnp.sum(jnp.where(eye, jnp.broadcast_to(i_row, (t, t)), 0.0), axis=1, keepdims=True)
    b_row = jnp.sum(jnp.where(tri_t, jnp.broadcast_to(lf_col, (t, t)), 0.0), axis=0, keepdims=True)
    m = m_ref[0:1, 0:1]
    dmat = jnp.where(tri, b_col - b_row + i_row, -jnp.inf)
    inter = b_col + m
    m_t = jnp.maximum(inter, jnp.max(dmat, axis=1, keepdims=True))
    wmat = jnp.exp(dmat - m_t)
    w_inter = jnp.exp(inter - m_t)
    s = lax.dot_general(q, k, (((1,), (1,)), ((), ())), preferred_element_type=F32) * wmat
    cb = c_ref[...].astype(BF16)
    num = jnp.dot(s.astype(BF16), v, preferred_element_type=F32) \
        + w_inter * jnp.dot(q, cb, preferred_element_type=F32)
    qn = jnp.sum(q.astype(F32) * n_ref[...], axis=1, keepdims=True)
    den = jnp.sum(s, axis=1, keepdims=True) + w_inter * qn
    h = num / jnp.maximum(jnp.abs(den), jnp.exp(-m_t))
    b_end = jnp.sum(lf, axis=1, keepdims=True)
    dlog = b_end - b_col + i_col
    m_new = jnp.maximum(b_end + m, jnp.max(dlog, axis=0, keepdims=True))
    wk = jnp.exp(dlog - m_new)
    decay = jnp.exp(b_end + m - m_new)
    kf = k.astype(F32) * wk
    c_ref[...] = decay * c_ref[...] + lax.dot_general(kf.astype(BF16), v, (((0,), (0,)), ((), ())),
                                                      preferred_element_type=F32)
    n_ref[...] = decay * n_ref[...] + jnp.sum(kf, axis=0, keepdims=True)
    m_ref[...] = jnp.broadcast_to(m_new, m_ref.shape)
    return h


def _scan_kernel(*refs, rev, final):
    qc, kc, vc, ic, fc, ql, kl, vl, il, fl = refs[:10]
    if final:
        hf_ref, o_ref, hng_ref, out_ref, c_ref, n_ref, m_ref = refs[10:]
    else:
        out_ref, c_ref, n_ref, m_ref = refs[10:]
    j = pl.program_id(2)
    masks = _scan_masks(CHUNK_T, rev)
    heads = [slice(hh * DH, (hh + 1) * DH) for hh in range(SCAN_HEADS)]

    def step(qr, kr, vr, ir, fr, hh):
        hs = heads[hh]
        return _scan_step(qr[0, :, hs], kr[0, :, hs], vr[0, :, hs], ir[hh], fr[hh],
                          c_ref.at[hh], n_ref.at[hh], m_ref.at[hh], masks)

    @pl.when(j == 0)
    def _():
        c_ref[...] = jnp.zeros(c_ref.shape, F32)
        n_ref[...] = jnp.zeros(n_ref.shape, F32)
        m_ref[...] = jnp.zeros(m_ref.shape, F32)
        for hh in range(SCAN_HEADS):
            step(qc, kc, vc, ic, fc, hh)

    @pl.when(j > 0)
    def _():
        for hh in range(SCAN_HEADS):
            h = step(ql, kl, vl, il, fl, hh)
            if final:
                hs = h + hf_ref[0, :, heads[hh]]
                mu = jnp.mean(hs, axis=-1, keepdims=True)
                hc = hs - mu
                var = jnp.mean(hc * hc, axis=-1, keepdims=True)
                hn = hc * lax.rsqrt(var + EPS) * hng_ref[:, heads[hh]]
                out_ref[0, :, heads[hh]] = (o_ref[0, :, heads[hh]] * hn.astype(BF16)).astype(BF16)
            else:
                out_ref[0, :, heads[hh]] = h


def _scan_call(qc, kc, vc, gc, ql, kl, vl, gl, hf, o, hng, *, nb, seq, rev):
    final = rev
    t = CHUNK_T
    ncl = seq // t
    if rev:
        cidx = lambda j: ncl - jnp.maximum(j, 1)
        irow, frow = 2 * NH, 3 * NH
    else:
        cidx = lambda j: jnp.maximum(j - 1, 0)
        irow, frow = 0, NH
    hp = SCAN_HEADS
    wid = hp * DH
    cspec = pl.BlockSpec((1, t, wid), lambda b, h, j: (b, 0, h))
    lspec = pl.BlockSpec((1, t, wid), lambda b, h, j: (b, cidx(j), h))
    gcs = lambda r0: pl.BlockSpec((hp, 1, t), lambda b, h, j: (r0 // hp + h, 0, b))
    gls = lambda r0: pl.BlockSpec((hp, 1, t), lambda b, h, j: (r0 // hp + h, 0, b * ncl + cidx(j)))
    in_specs = [cspec, cspec, cspec, gcs(irow), gcs(frow), lspec, lspec, lspec, gls(irow), gls(frow)]
    args = [qc, kc, vc, gc, gc, ql, kl, vl, gl, gl]
    if final:
        in_specs += [lspec, lspec, pl.BlockSpec((1, wid), lambda b, h, j: (0, h))]
        args += [hf, o, hng.reshape(1, E)]
    return pl.pallas_call(
        functools.partial(_scan_kernel, rev=rev, final=final),
        grid=(nb, NH // hp, ncl + 1),
        in_specs=in_specs,
        out_specs=lspec,
        out_shape=jax.ShapeDtypeStruct((nb, seq, E), BF16 if final else F32),
        scratch_shapes=[pltpu.VMEM((hp, DH, DH), F32), pltpu.VMEM((hp, 1, DH), F32),
                        pltpu.VMEM((hp, 8, LANES), F32)],
        compiler_params=_cparams(("parallel", "parallel", "arbitrary")),
        name="scan_bwd" if rev else "scan_fwd",
    )(*args)


def _m3_kernel(hg_ref, x_ref, mod_ref, wo_ref, gffn_ref, wrT_ref, br_ref, x1_ref, h2_ref, cls_ref, rw_ref):
    y = jnp.dot(hg_ref[...], wo_ref[...], preferred_element_type=F32)
    _tail(x_ref[...], y, mod_ref, gffn_ref, wrT_ref, br_ref, x1_ref, h2_ref, cls_ref, rw_ref)


def _m3_call(hg, x2d, mod, wo, gffn, wrT, br, *, seq, tm):
    n = x2d.shape[0]
    tps = seq // tm
    const = lambda shape: pl.BlockSpec(shape, lambda i: (0,) * len(shape))
    return pl.pallas_call(
        _m3_kernel,
        grid=(n // tm,),
        in_specs=[
            pl.BlockSpec((tm, E), lambda i: (i, 0)),
            pl.BlockSpec((tm, D), lambda i: (i, 0)),
            pl.BlockSpec((1, 8, D), lambda i: (i // tps, 0, 0)),
            const((E, D)), const((1, D)), const((N_EXPERTS, D)), const((N_EXPERTS, 1)),
        ],
        out_specs=[
            pl.BlockSpec((tm, D), lambda i: (i, 0)),
            pl.BlockSpec((tm * 8, LANES), lambda i: (i, 0)),
            pl.BlockSpec((8, tm), lambda i: (0, i)),
            pl.BlockSpec((8, tm), lambda i: (0, i)),
        ],
        out_shape=[
            jax.ShapeDtypeStruct((n, D), F32),
            jax.ShapeDtypeStruct((n * 8, LANES), F32),
            jax.ShapeDtypeStruct((8, n), I32),
            jax.ShapeDtypeStruct((8, n), F32),
        ],
        compiler_params=_cparams(("parallel",)),
        name="m3",
    )(hg, x2d, mod, wo, gffn.reshape(1, D), wrT, br.reshape(N_EXPERTS, 1))


def _final_kernel(x_ref, y_ref, mod_ref, g_ref, o_ref):
    x = x_ref[...] + mod_ref[0, 5:6, :] * _load_tiles(y_ref, x_ref.shape[0])
    ms = jnp.mean(x * x, axis=-1, keepdims=True)
    o_ref[...] = x * lax.rsqrt(ms + EPS) * g_ref[...]


def _final_call(x2d, y, mod, g, *, seq, tm):
    n = x2d.shape[0]
    tps = seq // tm
    return pl.pallas_call(
        _final_kernel,
        grid=(n // tm,),
        in_specs=[
            pl.BlockSpec((tm, D), lambda i: (i, 0)),
            pl.BlockSpec((tm * 8, LANES), lambda i: (i, 0)),
            pl.BlockSpec((1, 8, D), lambda i: (i // tps, 0, 0)),
            pl.BlockSpec((1, D), lambda i: (0, 0)),
        ],
        out_specs=pl.BlockSpec((tm, D), lambda i: (i, 0)),
        out_shape=jax.ShapeDtypeStruct((n, D), F32),
        compiler_params=_cparams(("parallel",)),
        name="final",
    )(x2d, y, mod, g.reshape(1, D))


def _blockdiag_tiles(w):
    per = MXU_DIM // 4
    w = w.reshape(E // MXU_DIM, per, 4, 4)
    eye = jnp.eye(per, dtype=w.dtype)
    t = w[:, :, :, None, :] * eye[None, :, None, :, None]
    return t.reshape(E // MXU_DIM, MXU_DIM, MXU_DIM)


def kernel(x, c, ctx, c_ctx, w_ada, b_ada, g_mix, g_ffn, g_final, w_in_ab, w_dw_a, b_dw_a, ln_g_a, ln_b_a, w_pool,
           b_pool, pool_scale, w_out_ab, w_in_c, w_conv_c, b_conv_c, w_q, w_k, w_v, w_gates, b_gates, hn_g, w_out_c,
           w_router, b_router, w1, w3, w2):
    nb, seq, _ = x.shape
    ctx_len = ctx.shape[1]
    assert ctx_len == CHUNK_T and seq % TM == 0 and seq % GRID_W == 0 and nb < 16
    n_lat = nb * seq
    n_ctx = nb * ctx_len
    x_lat = x.reshape(n_lat, D)
    x_ctx = ctx.reshape(n_ctx, D)

    s16 = jnp.zeros((16, D), F32).at[:nb].set(c).at[nb].set(c_ctx)
    mod = _mod_call(s16, w_ada, b_ada)
    pad8 = lambda m: jnp.pad(m.reshape(-1, 6, D), ((0, 0), (0, 2), (0, 0)))
    mod_lat = [pad8(mod[l, :nb]) for l in range(2)]
    mod_ctx = [pad8(mod[l, nb:nb + 1]) for l in range(2)]

    wrT = w_router.T
    bf = lambda w: w.astype(BF16)

    prm1 = (g_mix[0], bf(w_in_ab[0]), w_dw_a[0], b_dw_a[0], ln_g_a[0], ln_b_a[0])
    a_lat, bp_lat = _a1_call(x_lat, mod_lat[0], *prm1, seq=seq, tm=TM, seg=GRID_W)
    a_ctx, bp_ctx = _a1_call(x_ctx, mod_ctx[0], *prm1, seq=ctx_len, tm=ctx_len, seg=ctx_len)
    prm2 = (bf(w_pool[0]), b_pool[0], pool_scale[0], bf(w_out_ab[0]), g_ffn[0], wrT, b_router)
    n_tot = n_lat + n_ctx
    x1_lat, h2t, cls_lat, rw_lat = _a2_call(bp_lat, a_lat, x_lat, mod_lat[0], *prm2, None, seq=seq, tm=TM,
                                            stride=GRID_W, npos=seq // GRID_W, use_halo=True, h2_rows=n_tot,
                                            h2_row_off=0)
    x1_ctx, h2t, cls_ctx, rw_ctx = _a2_call(bp_ctx, a_ctx, x_ctx, mod_ctx[0], *prm2, h2t, seq=ctx_len, tm=ctx_len,
                                            stride=1, npos=ctx_len, use_halo=False, h2_rows=n_tot,
                                            h2_row_off=n_lat)
    cls0 = jnp.concatenate([cls_lat[0], cls_ctx[0]])
    rw0 = jnp.concatenate([rw_lat, rw_ctx], axis=1)
    w1b, w3b, w2b = (bf(w).reshape(-1, D, D) for w in (w1, w3, w2))
    y0 = _moe_call(h2t, cls0, rw0, w1b, w3b, w2b, 0)

    wgT = bf(w_gates[0].T)
    prm3 = (g_mix[1], bf(w_in_c[0]), w_conv_c[0], b_conv_c[0], bf(_blockdiag_tiles(w_q[0])),
            bf(_blockdiag_tiles(w_k[0])), bf(_blockdiag_tiles(w_v[0])), wgT, b_gates[0])
    x2_lat, ql, kl, vl, ol, gl = _m1_call(x1_lat, y0, mod_lat[0], mod_lat[1], *prm3, nb=nb, seq=seq, tm=TM1,
                                          y_row_off=0)
    _, qc, kc, vc, _, gc = _m1_call(x1_ctx, y0, mod_ctx[0], mod_ctx[1], *prm3, nb=nb, seq=ctx_len, tm=ctx_len,
                                    y_row_off=n_lat)
    r3 = lambda t, length: t.reshape(nb, length, E)
    ql, kl, vl, ol = (r3(t, seq) for t in (ql, kl, vl, ol))
    qc, kc, vc = (r3(t, ctx_len) for t in (qc, kc, vc))
    gl = gl.reshape(4 * NH, 1, n_lat)
    gc = gc.reshape(4 * NH, 1, n_ctx)
    hf = _scan_call(qc, kc, vc, gc, ql, kl, vl, gl, None, None, None, nb=nb, seq=seq, rev=False)
    hg = _scan_call(qc, kc, vc, gc, ql, kl, vl, gl, hf, ol, hn_g[0], nb=nb, seq=seq, rev=True)
    x3_lat, h2t1, cls1, rw1 = _m3_call(hg.reshape(n_lat, E), x2_lat, mod_lat[1], bf(w_out_c[0]), g_ffn[1], wrT,
                                       b_router, seq=seq, tm=TM)
    y1 = _moe_call(h2t1, cls1[0], rw1, w1b, w3b, w2b, 1)
    out = _final_call(x3_lat, y1, mod_lat[1], g_final, seq=seq, tm=TM)
    return out.reshape(nb, seq, D)
```

```python
import functools

import jax
import jax.numpy as jnp
from jax import lax
from jax.experimental import pallas as pl
from jax.experimental.pallas import tpu as pltpu

F32 = jnp.float32
BF16 = jnp.bfloat16
I32 = jnp.int32
HIGHEST = lax.Precision.HIGHEST

D = 1024
EPS = 1e-6
GRID_W = 64
CONV_K = 31
POOL_WINDOWS = (2, 4, 8, 16)
POOL_GC = 256
E = 2 * D
NH = 4
DH = E // NH
SHORT_K = 4
N_EXPERTS = 16
EPG = 4
N_CLASSES = 24
PAIR_LO = (0, 0, 0, 1, 1, 2)
PAIR_HI = (1, 2, 3, 2, 3, 3)

LANES = 128
MXU_DIM = 256
TM = 512
TM1 = 256
CHUNK_T = 256
SCAN_HEADS = 4
MOE_TB = 256
IDX_RING = 8
VMEM_LIMIT = 56 * 1024 * 1024


def _cparams(sem):
    return pltpu.CompilerParams(dimension_semantics=sem, vmem_limit_bytes=VMEM_LIMIT)


def _sigmoid(x):
    return jax.nn.sigmoid(x)


def _rms_mod(x, g, shift, scale):
    ms = jnp.mean(x * x, axis=-1, keepdims=True)
    return (x * lax.rsqrt(ms + EPS) * g) * (1.0 + scale) + shift


def _mod_kernel(s_ref, w_ref, b_ref, o_ref):
    s = s_ref[...]
    s = s * _sigmoid(s)
    o_ref[0] = jnp.dot(s, w_ref[0], preferred_element_type=F32, precision=HIGHEST) + b_ref[0]


def _mod_call(s16, w_ada, b_ada):
    depth = w_ada.shape[0]
    tn = 1536
    return pl.pallas_call(
        _mod_kernel,
        grid=(depth, 6 * D // tn),
        in_specs=[
            pl.BlockSpec((16, D), lambda l, n: (0, 0)),
            pl.BlockSpec((1, D, tn), lambda l, n: (l, 0, n)),
            pl.BlockSpec((1, 1, tn), lambda l, n: (l, 0, n)),
        ],
        out_specs=pl.BlockSpec((1, 16, tn), lambda l, n: (l, 0, n)),
        out_shape=jax.ShapeDtypeStruct((depth, 16, 6 * D), F32),
        compiler_params=_cparams(("parallel", "parallel")),
        name="mod",
    )(s16, w_ada, b_ada.reshape(depth, 1, 6 * D))


def _a1_kernel(x_ref, mod_ref, g_ref, win_ref, wdw_ref, bdw_ref, lng_ref, lnb_ref,
               a_ref, bp_ref, pad_ref, conv_ref, shf_ref, *, seg):
    tm = x_ref.shape[0]
    nseg = tm // seg
    h = _rms_mod(x_ref[...], g_ref[...], mod_ref[0, 0:1, :], mod_ref[0, 1:2, :])
    u = jnp.dot(h.astype(BF16), win_ref[...], preferred_element_type=F32)
    a = u[:, :D] * _sigmoid(u[:, D:2 * D])
    bp_ref[...] = u[:, 2 * D:]
    zeros16 = jnp.zeros((nseg, 16, D), F32)
    pad_ref[:, 0:16, :] = zeros16
    pad_ref[:, seg + 16:seg + 32, :] = zeros16
    pad_ref[:, 16:16 + seg, :] = a.reshape(nseg, seg, D)

    def seg_body(s, carry):
        for cs in range(D // LANES):
            sl = slice(cs * LANES, (cs + 1) * LANES)
            acc = jnp.zeros((seg, LANES), F32)
            for r in range(8):
                starts = [st for st in range(r, CONV_K + 1, 8) if st >= 1]
                lo = starts[0]
                nrow = starts[-1] - lo + seg
                if r:
                    shf_ref[r, 0:nrow, :] = pad_ref[s, pl.ds(lo, nrow), sl]
                    src = lambda st: shf_ref[r, st - lo:st - lo + seg, :]
                else:
                    src = lambda st: pad_ref[s, pl.ds(st, seg), sl]
                for st in starts:
                    acc = acc + wdw_ref[st - 1:st, sl] * src(st)
            conv_ref[pl.ds(pl.multiple_of(s * seg, seg), seg), sl] = acc + bdw_ref[:, sl]
        return carry

    lax.fori_loop(0, nseg, seg_body, 0)
    cv = conv_ref[...]
    mu = jnp.mean(cv, axis=-1, keepdims=True)
    cc = cv - mu
    var = jnp.mean(cc * cc, axis=-1, keepdims=True)
    y = cc * lax.rsqrt(var + EPS) * lng_ref[...] + lnb_ref[...]
    a_ref[...] = (y * _sigmoid(y)).astype(BF16)


def _a1_call(x2d, mod, g, win, wdw, bdw, lng, lnb, *, seq, tm, seg):
    n = x2d.shape[0]
    tps = seq // tm
    row = lambda v: v.reshape(1, -1)
    const = lambda shape: pl.BlockSpec(shape, lambda i: (0,) * len(shape))
    return pl.pallas_call(
        functools.partial(_a1_kernel, seg=seg),
        grid=(n // tm,),
        in_specs=[
            pl.BlockSpec((tm, D), lambda i: (i, 0)),
            pl.BlockSpec((1, 8, D), lambda i: ((i // tps) % mod.shape[0], 0, 0)),
            const((1, D)), const((D, 3 * D)), const((CONV_K, D)), const((1, D)), const((1, D)), const((1, D)),
        ],
        out_specs=[pl.BlockSpec((tm, D), lambda i: (i, 0)), pl.BlockSpec((tm, D), lambda i: (i, 0))],
        out_shape=[jax.ShapeDtypeStruct((n, D), BF16), jax.ShapeDtypeStruct((n, D), F32)],
        scratch_shapes=[pltpu.VMEM((tm // seg, seg + 32, D), F32), pltpu.VMEM((tm, D), F32),
                        pltpu.VMEM((8, seg + 24, LANES), F32)],
        compiler_params=_cparams(("parallel",)),
        name="a1",
    )(x2d, mod, row(g), win, wdw, row(bdw), row(lng), row(lnb))


def _store_tiles(ref, val):
    n = val.shape[0]
    for s in range(D // LANES):
        ref[pl.ds(s, n, stride=8), :] = val[:, s * LANES:(s + 1) * LANES]


def _load_tiles(ref, n):
    return jnp.concatenate([ref[pl.ds(s, n, stride=8), :] for s in range(D // LANES)], axis=-1)


def _tail(x, y, mod_ref, gffn_ref, wrT_ref, br_ref, x1_ref, h2_ref, cls_ref, rw_ref):
    tm = x.shape[0]
    x1 = x + mod_ref[0, 2:3, :] * y
    x1_ref[...] = x1
    h2 = _rms_mod(x1, gffn_ref[...], mod_ref[0, 3:4, :], mod_ref[0, 4:5, :])
    logits = lax.dot_general(wrT_ref[...], h2, (((1,), (1,)), ((), ())),
                             precision=HIGHEST, preferred_element_type=F32)
    sc = _sigmoid(logits)
    bi = sc + br_ref[...]
    gs = []
    for g in range(N_EXPERTS // EPG):
        v = [bi[EPG * g + j:EPG * g + j + 1, :] for j in range(EPG)]
        best = v[0] + v[1]
        for (p, q) in ((0, 2), (0, 3), (1, 2), (1, 3), (2, 3)):
            best = jnp.maximum(best, v[p] + v[q])
        gs.append(best)
    gsel = jnp.zeros((1, tm), I32)
    gbest = gs[0]
    for g in range(1, N_EXPERTS // EPG):
        better = gs[g] > gbest
        gsel = jnp.where(better, g, gsel)
        gbest = jnp.where(better, gs[g], gbest)

    def pick(arr, j):
        out = arr[j:j + 1, :]
        for g in range(1, N_EXPERTS // EPG):
            out = jnp.where(gsel == g, arr[EPG * g + j:EPG * g + j + 1, :], out)
        return out

    cand = [pick(bi, j) for j in range(EPG)]
    scs = [pick(sc, j) for j in range(EPG)]
    i1 = jnp.zeros((1, tm), I32)
    b1 = cand[0]
    for j in range(1, EPG):
        better = cand[j] > b1
        i1 = jnp.where(better, j, i1)
        b1 = jnp.where(better, cand[j], b1)
    neg = jnp.full((1, tm), -jnp.inf, F32)
    i2 = jnp.zeros((1, tm), I32)
    b2 = neg
    for j in range(EPG):
        cj = jnp.where(i1 == j, neg, cand[j])
        better = cj > b2
        i2 = jnp.where(better, j, i2)
        b2 = jnp.where(better, cj, b2)
    w1 = jnp.zeros((1, tm), F32)
    w2 = jnp.zeros((1, tm), F32)
    for j in range(EPG):
        w1 = jnp.where(i1 == j, scs[j], w1)
        w2 = jnp.where(i2 == j, scs[j], w2)
    den = w1 + w2
    w1 = w1 / den
    w2 = w2 / den
    first_lo = i1 < i2
    lo = jnp.minimum(i1, i2)
    hi = jnp.maximum(i1, i2)
    w_lo = jnp.where(first_lo, w1, w2)
    w_hi = jnp.where(first_lo, w2, w1)
    pidx = jnp.where(lo == 0, hi - 1, jnp.where(lo == 1, hi + 1, 5))
    cls = gsel * len(PAIR_LO) + pidx
    cls_ref[...] = jnp.broadcast_to(cls, (8, tm))
    rw_ref[...] = jnp.concatenate([w_lo, w_hi, jnp.zeros((6, tm), F32)], axis=0)
    _store_tiles(h2_ref, h2)


def _a2_kernel(bpp_ref, bpc_ref, bpn_ref, a_ref, x_ref, mod_ref, wpool_ref, bpool_ref, pscale_ref, wo_ref,
               gffn_ref, wrT_ref, br_ref, x1_ref, h2_ref, cls_ref, rw_ref, pb_ref, *, stride, npos, use_halo,
               ntile, nt):
    tm = x_ref.shape[0]
    halo = 8 * stride
    i = lax.rem(jnp.minimum(pl.program_id(0), ntile - 1), nt)
    cur = bpc_ref[...]
    if use_halo:
        pb_ref[0:halo, :] = jnp.where(i > 0, bpp_ref[...], 0.0)
        pb_ref[halo + tm:halo + tm + halo, :] = jnp.where(i < nt - 1, bpn_ref[...], 0.0)
    else:
        pb_ref[0:halo, :] = jnp.zeros((halo, D), F32)
        pb_ref[halo + tm:halo + tm + halo, :] = jnp.zeros((halo, D), F32)
    pb_ref[halo:halo + tm, :] = cur
    pos = (i * tm + lax.broadcasted_iota(I32, (tm, 1), 0)) // stride
    y = jnp.dot(a_ref[...], wo_ref[0:D, :], preferred_element_type=F32)
    for g, w in enumerate(POOL_WINDOWS):
        gsl = slice(g * POOL_GC, (g + 1) * POOL_GC)
        left = w // 2
        right = w - 1 - left
        tot = jnp.zeros((tm, POOL_GC), F32)
        for j in range(-left, right + 1):
            tot = tot + pb_ref[halo + j * stride:halo + j * stride + tm, gsl]
        cnt = (jnp.minimum(pos + right + 1, npos) - jnp.maximum(pos - left, 0)).astype(F32)
        pooled = tot / cnt - cur[:, gsl]
        pg = jnp.dot(pooled.astype(BF16), wpool_ref[g], preferred_element_type=F32) + bpool_ref[g:g + 1, :]
        pg = pg * pscale_ref[:, gsl]
        y = y + jnp.dot(pg.astype(BF16), wo_ref[D + g * POOL_GC:D + (g + 1) * POOL_GC, :],
                        preferred_element_type=F32)
    _tail(x_ref[...], y, mod_ref, gffn_ref, wrT_ref, br_ref, x1_ref, h2_ref, cls_ref, rw_ref)


def _a2_call(bp, a, x2d, mod, wpool, bpool, pscale, wo, gffn, wrT, br, h2_prev, *, seq, tm, stride, npos, use_halo,
             h2_rows, h2_row_off):
    n = x2d.shape[0]
    tps = seq // tm
    ntile = n // tm
    halo = 8 * stride
    off = h2_row_off // tm
    steps = ntile if h2_prev is not None else h2_rows // tm
    blk = lambda i: jnp.minimum(i, ntile - 1)
    row = lambda v: v.reshape(1, -1)
    const = lambda shape: pl.BlockSpec(shape, lambda i: (0,) * len(shape))
    in_specs = [
        pl.BlockSpec((tm, D), lambda i: (jnp.maximum(blk(i) - 1, 0), 0)),
        pl.BlockSpec((tm, D), lambda i: (blk(i), 0)),
        pl.BlockSpec((tm, D), lambda i: (jnp.minimum(blk(i) + 1, ntile - 1), 0)),
        pl.BlockSpec((tm, D), lambda i: (blk(i), 0)),
        pl.BlockSpec((tm, D), lambda i: (blk(i), 0)),
        pl.BlockSpec((1, 8, D), lambda i: ((blk(i) // tps) % mod.shape[0], 0, 0)),
        const((4, POOL_GC, POOL_GC)), const((4, POOL_GC)), const((1, D)), const((2 * D, D)),
        const((1, D)), const((N_EXPERTS, D)), const((N_EXPERTS, 1)),
    ]
    args = [bp, bp, bp, a, x2d, mod, wpool, bpool, row(pscale), wo, row(gffn), wrT, br.reshape(N_EXPERTS, 1)]
    aliases = {}
    body = functools.partial(_a2_kernel, stride=stride, npos=npos, use_halo=use_halo, ntile=ntile, nt=tps)
    if h2_prev is not None:
        in_specs.append(pl.BlockSpec(memory_space=pl.ANY))
        args.append(h2_prev)
        aliases = {len(args) - 1: 1}
        inner = body
        body = lambda *refs: inner(*refs[:13], *refs[14:])
    return pl.pallas_call(
        body,
        grid=(steps,),
        in_specs=in_specs,
        out_specs=[
            pl.BlockSpec((tm, D), lambda i: (blk(i), 0)),
            pl.BlockSpec((tm * 8, LANES), lambda i: (off + i, 0)),
            pl.BlockSpec((8, tm), lambda i: (0, blk(i))),
            pl.BlockSpec((8, tm), lambda i: (0, blk(i))),
        ],
        out_shape=[
            jax.ShapeDtypeStruct((n, D), F32),
            jax.ShapeDtypeStruct((h2_rows * 8, LANES), F32),
            jax.ShapeDtypeStruct((8, n), I32),
            jax.ShapeDtypeStruct((8, n), F32),
        ],
        scratch_shapes=[pltpu.VMEM((tm + 2 * halo, D), F32)],
        input_output_aliases=aliases,
        compiler_params=_cparams(("arbitrary",)),
        name="a2",
    )(*args)


def _moe_kernel(tlo_ref, thi_ref, tile_ref, r0_ref, r1_ref, idx_hbm, h2_hbm, rw_ref, w1lo, w3lo, w2lo, w1hi, w3hi,
                w2hi, y_hbm,
                idx_smem, xin, xb_ref, wt_ref, ybuf, sem_idx, sem_g, sem_s, *, tb, n_tiles):
    i = pl.program_id(0)
    last = n_tiles - 1
    unroll = 8

    def idx_copy(t):
        slot = lax.rem(t, IDX_RING)
        return pltpu.make_async_copy(idx_hbm.at[tile_ref[t]], idx_smem.at[slot], sem_idx.at[slot])

    def tile_rows(j):
        return pl.ds(j * 8 if isinstance(j, int) else pl.multiple_of(j * 8, 8), 8)

    def gather_copy(row8, slot2, j):
        return pltpu.make_async_copy(h2_hbm.at[pl.ds(pl.multiple_of(row8, 8), 8), :],
                                     xin.at[slot2, tile_rows(j), :], sem_g.at[slot2])

    def gather_all(slot2):
        return pltpu.make_async_copy(h2_hbm.at[pl.ds(0, tb * 8), :], xin.at[slot2], sem_g.at[slot2])

    def scatter_copy(row8, slot2, j):
        return pltpu.make_async_copy(ybuf.at[slot2, tile_rows(j), :],
                                     y_hbm.at[pl.ds(pl.multiple_of(row8, 8), 8), :], sem_s.at[slot2])

    def rows(fn, count=tb):
        def body(jj, c):
            for u in range(unroll):
                fn(jj * unroll + u)
            return c
        full = count // unroll
        lax.fori_loop(0, full, body, 0)
        if not isinstance(count, int):
            def tail(j, c):
                fn(j)
                return c
            lax.fori_loop(full * unroll, count, tail, 0)

    def wait_scatter(t):
        s2 = lax.rem(t, 2)
        rows(lambda j: scatter_copy(0, s2, j).wait(), r1_ref[t] - r0_ref[t])

    s2 = lax.rem(i, 2)

    @pl.when(i == 0)
    def _():
        for t in range(3):
            idx_copy(t).start()
        for t in range(2):
            idx_copy(t).wait()
            rows(lambda j: gather_copy(idx_smem[t, 0, j], t, j).start())

    @pl.when(i + 2 <= last)
    def _():
        idx_copy(i + 2).wait()

    @pl.when(i + 3 <= last)
    def _():
        idx_copy(i + 3).start()

    gather_all(s2).wait()

    @pl.when(i >= 2)
    def _():
        wait_scatter(i - 2)

    for s in range(D // LANES):
        xb_ref[:, s * LANES:(s + 1) * LANES] = xin[s2, pl.ds(s, tb, stride=8), :].astype(BF16)
    wt_ref[...] = jnp.zeros(wt_ref.shape, F32)
    wt_ref[0:8, :] = rw_ref[0]
    wcol = wt_ref[...].T
    sn = lax.rem(jnp.minimum(i + 2, last), IDX_RING)
    tp = jnp.maximum(i - 1, 0)
    r0p = r0_ref[tp]
    r1p = jnp.where(i >= 1, r1_ref[tp], r0p)
    sp = lax.rem(tp, IDX_RING)
    for j in range(tb):
        gather_copy(idx_smem[sn, 0, j], s2, j).start()
    for j in range(tb):
        @pl.when(jnp.logical_and(j >= r0p, j < r1p))
        def _():
            scatter_copy(idx_smem[sp, 0, j], 1 - s2, j).start(priority=j % 2)
    xb = xb_ref[...]

    def ffn(w1, w3, w2):
        h1 = jnp.dot(xb, w1[0], preferred_element_type=F32)
        h3 = jnp.dot(xb, w3[0], preferred_element_type=F32)
        act = (h1 * _sigmoid(h1)) * h3
        return jnp.dot(act.astype(BF16), w2[0], preferred_element_type=F32)

    y = wcol[:, 0:1] * ffn(w1lo, w3lo, w2lo) + wcol[:, 1:2] * ffn(w1hi, w3hi, w2hi)
    _store_tiles(ybuf.at[s2], y)

    @pl.when(i == last)
    def _():
        wait_scatter(i - 1)
        gather_all(0).wait()
        gather_all(1).wait()


def _moe_plan(cls, rw, n_tok, tb):
    nt = n_tok // tb
    classes = jnp.arange(N_CLASSES, dtype=I32)
    _, order, wlo, whi = lax.sort((cls, jnp.arange(n_tok, dtype=I32), rw[0], rw[1]), num_keys=1, is_stable=True)
    counts = jnp.sum((cls[:, None] == classes[None, :]).astype(I32), axis=0)
    cend = jnp.cumsum(counts)
    bounds = jnp.sort(jnp.concatenate([jnp.arange(nt + 1, dtype=I32) * tb, cend]))
    p0, p1 = bounds[:-1], bounds[1:]
    tile = jnp.minimum(p0 // tb, nt - 1)
    scls = jnp.minimum(jnp.sum((p0[:, None] >= cend[None, :]).astype(I32), axis=1), N_CLASSES - 1)
    r0 = p0 - tile * tb
    r1 = p1 - tile * tb
    grp = scls // len(PAIR_LO)
    pair = scls % len(PAIR_LO)
    tlo = grp * EPG + jnp.asarray(PAIR_LO, I32)[pair]
    thi = grp * EPG + jnp.asarray(PAIR_HI, I32)[pair]
    idx = (order * 8).reshape(nt, 1, tb)
    rw_sorted = jnp.pad(jnp.stack([wlo.reshape(nt, tb), whi.reshape(nt, tb)], axis=1), ((0, 0), (0, 6), (0, 0)))
    return [v.astype(I32) for v in (tlo, thi, tile, r0, r1)], idx, rw_sorted, nt + N_CLASSES


def _moe_call(h2t, cls, rw, w1, w3, w2, layer, *, tb=MOE_TB):
    n_tok = cls.shape[0]
    assert n_tok % tb == 0
    tables, idx, rw_sorted, n_steps = _moe_plan(cls, rw, n_tok, tb)
    tables[0] = tables[0] + layer * N_EXPERTS
    tables[1] = tables[1] + layer * N_EXPERTS
    wspec_lo = pl.BlockSpec((1, D, D), lambda i, lo, hi, tile, r0, r1: (lo[i], 0, 0))
    wspec_hi = pl.BlockSpec((1, D, D), lambda i, lo, hi, tile, r0, r1: (hi[i], 0, 0))
    grid_spec = pltpu.PrefetchScalarGridSpec(
        num_scalar_prefetch=5,
        grid=(n_steps,),
        in_specs=[pl.BlockSpec(memory_space=pl.ANY), pl.BlockSpec(memory_space=pl.ANY),
                  pl.BlockSpec((1, 8, tb), lambda i, lo, hi, tile, r0, r1: (tile[i], 0, 0)),
                  wspec_lo, wspec_lo, wspec_lo, wspec_hi, wspec_hi, wspec_hi],
        out_specs=pl.BlockSpec(memory_space=pl.ANY),
        scratch_shapes=[
            pltpu.SMEM((IDX_RING, 1, tb), I32),
            pltpu.VMEM((2, tb * 8, LANES), F32),
            pltpu.VMEM((tb, D), BF16),
            pltpu.VMEM((LANES, tb), F32),
            pltpu.VMEM((2, tb * 8, LANES), F32),
            pltpu.SemaphoreType.DMA((IDX_RING,)),
            pltpu.SemaphoreType.DMA((2,)),
            pltpu.SemaphoreType.DMA((2,)),
        ],
    )
    return pl.pallas_call(
        functools.partial(_moe_kernel, tb=tb, n_tiles=n_steps),
        grid_spec=grid_spec,
        out_shape=jax.ShapeDtypeStruct((n_tok * 8, LANES), F32),
        compiler_params=_cparams(("arbitrary",)),
        name="moe",
    )(*tables, idx, h2t, rw_sorted, w1, w3, w2, w1, w3, w2)


def _m1_kernel(xp_ref, xc_ref, xn_ref, yp_ref, yc_ref, yn_ref, modp_ref, modc_ref, g_ref, win_ref, wconv_ref,
               bconv_ref, bdq_ref, bdk_ref, bdv_ref, wgT_ref, bg_ref,
               x2_ref, q_ref, k_ref, v_ref, o_ref, gT_ref, xm_ref):
    tm = xc_ref.shape[0]
    i = pl.program_id(1)
    nt = pl.num_programs(1)
    xe = jnp.concatenate([xp_ref[...], xc_ref[...], xn_ref[...]], axis=0)
    ye = jnp.concatenate([_load_tiles(yp_ref, 8), _load_tiles(yc_ref, tm), _load_tiles(yn_ref, 8)], axis=0)
    xe = xe + modp_ref[0, 5:6, :] * ye
    x2_ref[...] = xe[8:8 + tm]
    h = _rms_mod(xe, g_ref[...], modc_ref[0, 0:1, :], modc_ref[0, 1:2, :]).astype(BF16)
    xm = jnp.dot(h, win_ref[:, 0:E], preferred_element_type=F32)
    z = jnp.dot(h, win_ref[:, E:2 * E], preferred_element_type=F32)
    o_ref[...] = _sigmoid(z[8:8 + tm]).astype(BF16)
    rid = lax.broadcasted_iota(I32, (tm + 16, 1), 0)
    inside = jnp.logical_and(jnp.logical_or(rid >= 8, i > 0), jnp.logical_or(rid < tm + 8, i < nt - 1))
    xm = jnp.where(inside, xm, 0.0)
    xm_ref[...] = xm
    acc = jnp.zeros((tm, E), F32) + bconv_ref[...]
    for kk in range(SHORT_K):
        acc = acc + wconv_ref[kk:kk + 1, :] * xm_ref[6 + kk:6 + kk + tm, :]
    xcb = (acc * _sigmoid(acc)).astype(BF16)
    xmb = xm[8:8 + tm].astype(BF16)
    nt_dims = (((1,), (1,)), ((), ()))
    gT = jnp.zeros((4 * NH, tm), F32) + bg_ref[...]
    kscale = DH ** -0.5
    for j in range(E // MXU_DIM):
        sl = slice(j * MXU_DIM, (j + 1) * MXU_DIM)
        qj = jnp.dot(xcb[:, sl], bdq_ref[j], preferred_element_type=F32)
        kj = jnp.dot(xcb[:, sl], bdk_ref[j], preferred_element_type=F32)
        vj = jnp.dot(xmb[:, sl], bdv_ref[j], preferred_element_type=F32)
        qb, kb, vb = qj.astype(BF16), kj.astype(BF16), vj.astype(BF16)
        gT = gT + lax.dot_general(wgT_ref[:, sl], qb, nt_dims, preferred_element_type=F32)
        gT = gT + lax.dot_general(wgT_ref[:, E + j * MXU_DIM:E + (j + 1) * MXU_DIM], kb, nt_dims,
                                  preferred_element_type=F32)
        gT = gT + lax.dot_general(wgT_ref[:, 2 * E + j * MXU_DIM:2 * E + (j + 1) * MXU_DIM], vb, nt_dims,
                                  preferred_element_type=F32)
        q_ref[:, sl] = qb
        k_ref[:, sl] = (kj * kscale).astype(BF16)
        v_ref[:, sl] = vb
    gT_ref[...] = gT


def _m1_call(xa, ya, modp, modc, g, win, wconv, bconv, bdq, bdk, bdv, wgT, bg, *, nb, seq, tm, y_row_off):
    n = nb * seq
    tps = seq // tm
    hb = tm // 8
    nb8 = n // 8
    yo = y_row_off // tm
    yo8 = y_row_off // 8
    blk = lambda b, i: b * tps + i
    const = lambda shape: pl.BlockSpec(shape, lambda b, i: (0,) * len(shape))
    prev8 = lambda b, i: jnp.maximum(blk(b, i) * hb - 1, 0)
    next8 = lambda b, i: jnp.minimum((blk(b, i) + 1) * hb, nb8 - 1)
    in_specs = [
        pl.BlockSpec((8, D), lambda b, i: (prev8(b, i), 0)),
        pl.BlockSpec((tm, D), lambda b, i: (blk(b, i), 0)),
        pl.BlockSpec((8, D), lambda b, i: (next8(b, i), 0)),
        pl.BlockSpec((8 * 8, LANES), lambda b, i: (yo8 + prev8(b, i), 0)),
        pl.BlockSpec((tm * 8, LANES), lambda b, i: (yo + blk(b, i), 0)),
        pl.BlockSpec((8 * 8, LANES), lambda b, i: (yo8 + next8(b, i), 0)),
        pl.BlockSpec((1, 8, D), lambda b, i: (b % modp.shape[0], 0, 0)),
        pl.BlockSpec((1, 8, D), lambda b, i: (b % modc.shape[0], 0, 0)),
        const((1, D)), const((D, 2 * E)), const((SHORT_K, E)), const((1, E)),
        const((E // MXU_DIM, MXU_DIM, MXU_DIM)), const((E // MXU_DIM, MXU_DIM, MXU_DIM)),
        const((E // MXU_DIM, MXU_DIM, MXU_DIM)), const((4 * NH, 3 * E)), const((4 * NH, 1)),
    ]
    tok = lambda w, dt: jax.ShapeDtypeStruct((n, w), dt)
    tspec = lambda w: pl.BlockSpec((tm, w), lambda b, i: (blk(b, i), 0))
    return pl.pallas_call(
        _m1_kernel,
        grid=(nb, tps),
        in_specs=in_specs,
        out_specs=[tspec(D), tspec(E), tspec(E), tspec(E), tspec(E),
                   pl.BlockSpec((4 * NH, tm), lambda b, i: (0, blk(b, i)))],
        out_shape=[tok(D, F32), tok(E, BF16), tok(E, BF16), tok(E, BF16), tok(E, BF16),
                   jax.ShapeDtypeStruct((4 * NH, n), F32)],
        scratch_shapes=[pltpu.VMEM((tm + 16, E), F32)],
        compiler_params=_cparams(("parallel", "arbitrary")),
        name="m1",
    )(xa, xa, xa, ya, ya, ya, modp, modc, g.reshape(1, D), win, wconv, bconv.reshape(1, E), bdq, bdk, bdv, wgT,
      bg.reshape(4 * NH, 1))


def _log_sigmoid(x):
    return jnp.minimum(x, 0.0) - jnp.log1p(jnp.exp(-jnp.abs(x)))


def _scan_masks(t, rev):
    r = lax.broadcasted_iota(I32, (t, t), 0)
    c = lax.broadcasted_iota(I32, (t, t), 1)
    tri = (c >= r) if rev else (c <= r)
    tri_t = (r >= c) if rev else (r <= c)
    return tri, tri_t, c == r


def _scan_step(q, k, v, i_row, f_row, c_ref, n_ref, m_ref, masks):
    t = q.shape[0]
    tri, tri_t, eye = masks
    lf = _log_sigmoid(f_row)
    lf_b = jnp.broadcast_to(lf, (t, t))
    b_col = jnp.sum(jnp.where(tri, lf_b, 0.0), axis=1, keepdims=True)
    lf_col = jnp.sum(jnp.where(eye, lf_b, 0.0), axis=1, keepdims=True)
    i_col = jnp.sum(jnp.where(eye, jnp.broadcast_to(i_row, (t, t)), 0.0), axis=1, keepdims=True)
    b_row = jnp.sum(jnp.where(tri_t, jnp.broadcast_to(lf_col, (t, t)), 0.0), axis=0, keepdims=True)
    m = m_ref[0:1, 0:1]
    dmat = jnp.where(tri, b_col - b_row + i_row, -jnp.inf)
    inter = b_col + m
    m_t = jnp.maximum(inter, jnp.max(dmat, axis=1, keepdims=True))
    wmat = jnp.exp(dmat - m_t)
    w_inter = jnp.exp(inter - m_t)
    s = lax.dot_general(q, k, (((1,), (1,)), ((), ())), preferred_element_type=F32) * wmat
    cb = c_ref[...].astype(BF16)
    num = jnp.dot(s.astype(BF16), v, preferred_element_type=F32) \
        + w_inter * jnp.dot(q, cb, preferred_element_type=F32)
    qn = jnp.sum(q.astype(F32) * n_ref[...], axis=1, keepdims=True)
    den = jnp.sum(s, axis=1, keepdims=True) + w_inter * qn
    h = num / jnp.maximum(jnp.abs(den), jnp.exp(-m_t))
    b_end = jnp.sum(lf, axis=1, keepdims=True)
    dlog = b_end - b_col + i_col
    m_new = jnp.maximum(b_end + m, jnp.max(dlog, axis=0, keepdims=True))
    wk = jnp.exp(dlog - m_new)
    decay = jnp.exp(b_end + m - m_new)
    kf = k.astype(F32) * wk
    c_ref[...] = decay * c_ref[...] + lax.dot_general(kf.astype(BF16), v, (((0,), (0,)), ((), ())),
                                                      preferred_element_type=F32)
    n_ref[...] = decay * n_ref[...] + jnp.sum(kf, axis=0, keepdims=True)
    m_ref[...] = jnp.broadcast_to(m_new, m_ref.shape)
    return h


def _scan_kernel(*refs, rev, final):
    qc, kc, vc, ic, fc, ql, kl, vl, il, fl = refs[:10]
    if final:
        hf_ref, o_ref, hng_ref, out_ref, c_ref, n_ref, m_ref = refs[10:]
    else:
        out_ref, c_ref, n_ref, m_ref = refs[10:]
    j = pl.program_id(2)
    masks = _scan_masks(CHUNK_T, rev)
    heads = [slice(hh * DH, (hh + 1) * DH) for hh in range(SCAN_HEADS)]

    def step(qr, kr, vr, ir, fr, hh):
        hs = heads[hh]
        return _scan_step(qr[0, :, hs], kr[0, :, hs], vr[0, :, hs], ir[hh], fr[hh],
                          c_ref.at[hh], n_ref.at[hh], m_ref.at[hh], masks)

    @pl.when(j == 0)
    def _():
        c_ref[...] = jnp.zeros(c_ref.shape, F32)
        n_ref[...] = jnp.zeros(n_ref.shape, F32)
        m_ref[...] = jnp.zeros(m_ref.shape, F32)
        for hh in range(SCAN_HEADS):
            step(qc, kc, vc, ic, fc, hh)

    @pl.when(j > 0)
    def _():
        for hh in range(SCAN_HEADS):
            h = step(ql, kl, vl, il, fl, hh)
            if final:
                hs = h + hf_ref[0, :, heads[hh]]
                mu = jnp.mean(hs, axis=-1, keepdims=True)
                hc = hs - mu
                var = jnp.mean(hc * hc, axis=-1, keepdims=True)
                hn = hc * lax.rsqrt(var + EPS) * hng_ref[:, heads[hh]]
                out_ref[0, :, heads[hh]] = (o_ref[0, :, heads[hh]] * hn.astype(BF16)).astype(BF16)
            else:
                out_ref[0, :, heads[hh]] = h


def _scan_call(qc, kc, vc, gc, ql, kl, vl, gl, hf, o, hng, *, nb, seq, rev):
    final = rev
    t = CHUNK_T
    ncl = seq // t
    if rev:
        cidx = lambda j: ncl - jnp.maximum(j, 1)
        irow, frow = 2 * NH, 3 * NH
    else:
        cidx = lambda j: jnp.maximum(j - 1, 0)
        irow, frow = 0, NH
    hp = SCAN_HEADS
    wid = hp * DH
    cspec = pl.BlockSpec((1, t, wid), lambda b, h, j: (b, 0, h))
    lspec = pl.BlockSpec((1, t, wid), lambda b, h, j: (b, cidx(j), h))
    gcs = lambda r0: pl.BlockSpec((hp, 1, t), lambda b, h, j: (r0 // hp + h, 0, b))
    gls = lambda r0: pl.BlockSpec((hp, 1, t), lambda b, h, j: (r0 // hp + h, 0, b * ncl + cidx(j)))
    in_specs = [cspec, cspec, cspec, gcs(irow), gcs(frow), lspec, lspec, lspec, gls(irow), gls(frow)]
    args = [qc, kc, vc, gc, gc, ql, kl, vl, gl, gl]
    if final:
        in_specs += [lspec, lspec, pl.BlockSpec((1, wid), lambda b, h, j: (0, h))]
        args += [hf, o, hng.reshape(1, E)]
    return pl.pallas_call(
        functools.partial(_scan_kernel, rev=rev, final=final),
        grid=(nb, NH // hp, ncl + 1),
        in_specs=in_specs,
        out_specs=lspec,
        out_shape=jax.ShapeDtypeStruct((nb, seq, E), BF16 if final else F32),
        scratch_shapes=[pltpu.VMEM((hp, DH, DH), F32), pltpu.VMEM((hp, 1, DH), F32),
                        pltpu.VMEM((hp, 8, LANES), F32)],
        compiler_params=_cparams(("parallel", "parallel", "arbitrary")),
        name="scan_bwd" if rev else "scan_fwd",
    )(*args)


def _m3_kernel(hg_ref, x_ref, mod_ref, wo_ref, gffn_ref, wrT_ref, br_ref, x1_ref, h2_ref, cls_ref, rw_ref):
    y = jnp.dot(hg_ref[...], wo_ref[...], preferred_element_type=F32)
    _tail(x_ref[...], y, mod_ref, gffn_ref, wrT_ref, br_ref, x1_ref, h2_ref, cls_ref, rw_ref)


def _m3_call(hg, x2d, mod, wo, gffn, wrT, br, *, seq, tm):
    n = x2d.shape[0]
    tps = seq // tm
    const = lambda shape: pl.BlockSpec(shape, lambda i: (0,) * len(shape))
    return pl.pallas_call(
        _m3_kernel,
        grid=(n // tm,),
        in_specs=[
            pl.BlockSpec((tm, E), lambda i: (i, 0)),
            pl.BlockSpec((tm, D), lambda i: (i, 0)),
            pl.BlockSpec((1, 8, D), lambda i: (i // tps, 0, 0)),
            const((E, D)), const((1, D)), const((N_EXPERTS, D)), const((N_EXPERTS, 1)),
        ],
        out_specs=[
            pl.BlockSpec((tm, D), lambda i: (i, 0)),
            pl.BlockSpec((tm * 8, LANES), lambda i: (i, 0)),
            pl.BlockSpec((8, tm), lambda i: (0, i)),
            pl.BlockSpec((8, tm), lambda i: (0, i)),
        ],
        out_shape=[
            jax.ShapeDtypeStruct((n, D), F32),
            jax.ShapeDtypeStruct((n * 8, LANES), F32),
            jax.ShapeDtypeStruct((8, n), I32),
            jax.ShapeDtypeStruct((8, n), F32),
        ],
        compiler_params=_cparams(("parallel",)),
        name="m3",
    )(hg, x2d, mod, wo, gffn.reshape(1, D), wrT, br.reshape(N_EXPERTS, 1))


def _final_kernel(x_ref, y_ref, mod_ref, g_ref, o_ref):
    x = x_ref[...] + mod_ref[0, 5:6, :] * _load_tiles(y_ref, x_ref.shape[0])
    ms = jnp.mean(x * x, axis=-1, keepdims=True)
    o_ref[...] = x * lax.rsqrt(ms + EPS) * g_ref[...]


def _final_call(x2d, y, mod, g, *, seq, tm):
    n = x2d.shape[0]
    tps = seq // tm
    return pl.pallas_call(
        _final_kernel,
        grid=(n // tm,),
        in_specs=[
            pl.BlockSpec((tm, D), lambda i: (i, 0)),
            pl.BlockSpec((tm * 8, LANES), lambda i: (i, 0)),
            pl.BlockSpec((1, 8, D), lambda i: (i // tps, 0, 0)),
            pl.BlockSpec((1, D), lambda i: (0, 0)),
        ],
        out_specs=pl.BlockSpec((tm, D), lambda i: (i, 0)),
        out_shape=jax.ShapeDtypeStruct((n, D), F32),
        compiler_params=_cparams(("parallel",)),
        name="final",
    )(x2d, y, mod, g.reshape(1, D))


def _blockdiag_tiles(w):
    per = MXU_DIM // 4
    w = w.reshape(E // MXU_DIM, per, 4, 4)
    eye = jnp.eye(per, dtype=w.dtype)
    t = w[:, :, :, None, :] * eye[None, :, None, :, None]
    return t.reshape(E // MXU_DIM, MXU_DIM, MXU_DIM)


def kernel(x, c, ctx, c_ctx, w_ada, b_ada, g_mix, g_ffn, g_final, w_in_ab, w_dw_a, b_dw_a, ln_g_a, ln_b_a, w_pool,
           b_pool, pool_scale, w_out_ab, w_in_c, w_conv_c, b_conv_c, w_q, w_k, w_v, w_gates, b_gates, hn_g, w_out_c,
           w_router, b_router, w1, w3, w2):
    nb, seq, _ = x.shape
    ctx_len = ctx.shape[1]
    assert ctx_len == CHUNK_T and seq % TM == 0 and seq % GRID_W == 0 and nb < 16
    n_lat = nb * seq
    n_ctx = nb * ctx_len
    x_lat = x.reshape(n_lat, D)
    x_ctx = ctx.reshape(n_ctx, D)

    s16 = jnp.zeros((16, D), F32).at[:nb].set(c).at[nb].set(c_ctx)
    mod = _mod_call(s16, w_ada, b_ada)
    pad8 = lambda m: jnp.pad(m.reshape(-1, 6, D), ((0, 0), (0, 2), (0, 0)))
    mod_lat = [pad8(mod[l, :nb]) for l in range(2)]
    mod_ctx = [pad8(mod[l, nb:nb + 1]) for l in range(2)]

    wrT = w_router.T
    bf = lambda w: w.astype(BF16)

    prm1 = (g_mix[0], bf(w_in_ab[0]), w_dw_a[0], b_dw_a[0], ln_g_a[0], ln_b_a[0])
    a_lat, bp_lat = _a1_call(x_lat, mod_lat[0], *prm1, seq=seq, tm=TM, seg=GRID_W)
    a_ctx, bp_ctx = _a1_call(x_ctx, mod_ctx[0], *prm1, seq=ctx_len, tm=ctx_len, seg=ctx_len)
    prm2 = (bf(w_pool[0]), b_pool[0], pool_scale[0], bf(w_out_ab[0]), g_ffn[0], wrT, b_router)
    n_tot = n_lat + n_ctx
    x1_lat, h2t, cls_lat, rw_lat = _a2_call(bp_lat, a_lat, x_lat, mod_lat[0], *prm2, None, seq=seq, tm=TM,
                                            stride=GRID_W, npos=seq // GRID_W, use_halo=True, h2_rows=n_tot,
                                            h2_row_off=0)
    x1_ctx, h2t, cls_ctx, rw_ctx = _a2_call(bp_ctx, a_ctx, x_ctx, mod_ctx[0], *prm2, h2t, seq=ctx_len, tm=ctx_len,
                                            stride=1, npos=ctx_len, use_halo=False, h2_rows=n_tot,
                                            h2_row_off=n_lat)
    cls0 = jnp.concatenate([cls_lat[0], cls_ctx[0]])
    rw0 = jnp.concatenate([rw_lat, rw_ctx], axis=1)
    w1b, w3b, w2b = (bf(w).reshape(-1, D, D) for w in (w1, w3, w2))
    y0 = _moe_call(h2t, cls0, rw0, w1b, w3b, w2b, 0)

    wgT = bf(w_gates[0].T)
    prm3 = (g_mix[1], bf(w_in_c[0]), w_conv_c[0], b_conv_c[0], bf(_blockdiag_tiles(w_q[0])),
            bf(_blockdiag_tiles(w_k[0])), bf(_blockdiag_tiles(w_v[0])), wgT, b_gates[0])
    x2_lat, ql, kl, vl, ol, gl = _m1_call(x1_lat, y0, mod_lat[0], mod_lat[1], *prm3, nb=nb, seq=seq, tm=TM1,
                                          y_row_off=0)
    _, qc, kc, vc, _, gc = _m1_call(x1_ctx, y0, mod_ctx[0], mod_ctx[1], *prm3, nb=nb, seq=ctx_len, tm=ctx_len,
                                    y_row_off=n_lat)
    r3 = lambda t, length: t.reshape(nb, length, E)
    ql, kl, vl, ol = (r3(t, seq) for t in (ql, kl, vl, ol))
    qc, kc, vc = (r3(t, ctx_len) for t in (qc, kc, vc))
    gl = gl.reshape(4 * NH, 1, n_lat)
    gc = gc.reshape(4 * NH, 1, n_ctx)
    hf = _scan_call(qc, kc, vc, gc, ql, kl, vl, gl, None, None, None, nb=nb, seq=seq, rev=False)
    hg = _scan_call(qc, kc, vc, gc, ql, kl, vl, gl, hf, ol, hn_g[0], nb=nb, seq=seq, rev=True)
    x3_lat, h2t1, cls1, rw1 = _m3_call(hg.reshape(n_lat, E), x2_lat, mod_lat[1], bf(w_out_c[0]), g_ffn[1], wrT,
                                       b_router, seq=seq, tm=TM)
    y1 = _moe_call(h2t1, cls1[0], rw1, w1b, w3b, w2b, 1)
    out = _final_call(x3_lat, y1, mod_lat[1], g_final, seq=seq, tm=TM)
    return out.reshape(nb, seq, D)
```
